```python
import jax, jax.numpy as jnp
from jax import lax
import numpy as np

D_MODEL = 1024
BATCH = 32
SEQ = 2048
DEPTH = 4
DEC_BATCH = 16
DEC_SEQ = 64
PAST_LEN = 2048

CHUNK = 64
HEAD_DIM = 64
A_HEADS = D_MODEL // (2 * HEAD_DIM)
B_HEADS = D_MODEL // (2 * HEAD_DIM)
A_WIDTH = A_HEADS * HEAD_DIM
B_WIDTH = B_HEADS * HEAD_DIM
MIX_WIDTH = A_WIDTH + B_WIDTH
A_PAST_CHUNKS = 8
A_PAST = A_PAST_CHUNKS * CHUNK
A_BAND = A_PAST + CHUNK
REL_CLIP = 128
N_REL = 2 * REL_CLIP + 1
IDX_HEADS = 8
IDX_DIM = 64
TOPK_MAX = 256
Q_BLOCK = CHUNK
ROPE_THETA = 10000.0
D_FF = 4 * D_MODEL
LN_EPS = 1e-5
DN_ALPHA = (2 * DEPTH) ** 0.25
DN_BETA = (8 * DEPTH) ** -0.25
IN_SIZES = (A_WIDTH, A_WIDTH, A_WIDTH, B_WIDTH, B_WIDTH, B_WIDTH, IDX_HEADS * IDX_DIM, IDX_DIM, IDX_HEADS)
IN_WIDTH = 3 * A_WIDTH + 3 * B_WIDTH + IDX_HEADS * IDX_DIM + IDX_DIM + IDX_HEADS

kernel_name = 'hybrid_chunkband_dsa_streaming_encoder_step'


def layer_norm(x, g, b):
    xf = x.astype(jnp.float32)
    mu = jnp.mean(xf, -1, keepdims=True)
    var = jnp.mean(jnp.square(xf - mu), -1, keepdims=True)
    y = (xf - mu) * lax.rsqrt(var + LN_EPS)
    return (y * g.astype(jnp.float32) + b.astype(jnp.float32)).astype(x.dtype)


def rope(x, pos):
    half = x.shape[-1] // 2
    inv = ROPE_THETA ** (-jnp.arange(half, dtype=jnp.float32) / half)
    ang = pos.astype(jnp.float32)[:, None] * inv[None, :]
    shp = (pos.shape[0],) + (1,) * (x.ndim - 3) + (half,)
    cos = jnp.cos(ang).reshape(shp)
    sin = jnp.sin(ang).reshape(shp)
    xf = x.astype(jnp.float32)
    x1, x2 = xf[..., :half], xf[..., half:]
    return jnp.concatenate([x1 * cos - x2 * sin, x2 * cos + x1 * sin], -1).astype(x.dtype)


def project(x, w_in, pos):
    bsz, t = x.shape[0], x.shape[1]
    h = jnp.einsum('btd,de->bte', x, w_in)
    cuts = np.cumsum(IN_SIZES)[:-1].tolist()
    qa, ka, va, qb, kb, vb, qi, ki, wi = jnp.split(h, cuts, axis=-1)
    qa = qa.reshape(bsz, t, A_HEADS, HEAD_DIM)
    ka = ka.reshape(bsz, t, A_HEADS, HEAD_DIM)
    va = va.reshape(bsz, t, A_HEADS, HEAD_DIM)
    qb = rope(qb.reshape(bsz, t, B_HEADS, HEAD_DIM), pos)
    kb = rope(kb.reshape(bsz, t, B_HEADS, HEAD_DIM), pos)
    vb = vb.reshape(bsz, t, B_HEADS, HEAD_DIM)
    qi = rope(qi.reshape(bsz, t, IDX_HEADS, IDX_DIM), pos)
    ki = rope(ki, pos)
    wi = wi * (IDX_HEADS * IDX_DIM) ** -0.5
    return qa, ka, va, qb, kb, vb, qi, ki, wi


def band_bias(table, n_q, n_past):
    dist = jnp.arange(n_q)[:, None] + n_past - jnp.arange(n_past + n_q)[None, :]
    idx = jnp.clip(dist, -REL_CLIP, REL_CLIP) + REL_CLIP
    return table[:, idx].astype(jnp.float32)


def band_attention_prompt(q, k, v, table):
    bsz, s_len, nh, hd = q.shape
    nc = s_len // CHUNK
    pad = jnp.zeros((bsz, A_PAST, nh, hd), k.dtype)
    kp = jnp.concatenate([pad, k], 1)
    vp = jnp.concatenate([pad, v], 1)
    bias = band_bias(table, CHUNK, A_PAST)
    qc = q.reshape(bsz, nc, CHUNK, nh, hd).transpose(1, 0, 2, 3, 4)
    scale = hd ** -0.5

    def one_chunk(args):
        c, qblk = args
        start = c * CHUNK
        kb = lax.dynamic_slice_in_dim(kp, start, A_BAND, axis=1)
        vb = lax.dynamic_slice_in_dim(vp, start, A_BAND, axis=1)
        s = jnp.einsum('bqhd,bkhd->bhqk', qblk, kb, preferred_element_type=jnp.float32) * scale + bias
        kpos = start - A_PAST + jnp.arange(A_BAND)
        s = jnp.where(kpos[None, None, None, :] >= 0, s, -jnp.inf)
        p = jax.nn.softmax(s, axis=-1).astype(vb.dtype)
        return jnp.einsum('bhqk,bkhd->bqhd', p, vb)

    o = lax.map(one_chunk, (jnp.arange(nc), qc))
    return o.transpose(1, 0, 2, 3, 4).reshape(bsz, s_len, nh * hd)


def band_attention_step(q, k_new, v_new, k_cache, v_cache, table):
    bsz, t, nh, hd = q.shape
    p_len = k_cache.shape[1]
    kk = jnp.concatenate([k_cache, k_new], 1)
    vv = jnp.concatenate([v_cache, v_new], 1)
    bias = band_bias(table, t, p_len)
    s = jnp.einsum('bqhd,bkhd->bhqk', q, kk, preferred_element_type=jnp.float32) * hd ** -0.5 + bias
    p = jax.nn.softmax(s, axis=-1).astype(vv.dtype)
    return jnp.einsum('bhqk,bkhd->bqhd', p, vv).reshape(bsz, t, nh * hd)


def sparse_attend(q, k, v, qi, w, ki, q_pos, k_pos, topk):
    adm = (k_pos[None, :] // CHUNK) <= (q_pos[:, None] // CHUNK)
    dots = jnp.einsum('qhd,ld->qhl', qi, ki, preferred_element_type=jnp.float32)
    score = jnp.einsum('qh,qhl->ql', w.astype(jnp.float32), jax.nn.relu(dots))
    score = jnp.where(adm, score, -jnp.inf)
    _, idx = lax.top_k(score, topk)
    valid = jnp.take_along_axis(adm, idx, axis=1)
    kg = k[idx]
    vg = v[idx]
    s = jnp.einsum('qhd,qkhd->qhk', q, kg, preferred_element_type=jnp.float32) * q.shape[-1] ** -0.5
    s = jnp.where(valid[:, None, :], s, -jnp.inf)
    p = jax.nn.softmax(s, axis=-1).astype(vg.dtype)
    return jnp.einsum('qhk,qkhd->qhd', p, vg)


def sparse_attention_prompt(q, k, v, qi, w, ki, pos, topk):
    bsz, s_len, nh, hd = q.shape
    nb = s_len // Q_BLOCK
    pos_blk = pos.reshape(nb, Q_BLOCK)

    def per_seq(args):
        qs, ks, vs, qis, ws, kis = args
        def per_block(a):
            bq, bqi, bw, bpos = a
            return sparse_attend(bq, ks, vs, bqi, bw, kis, bpos, pos, topk)
        o = lax.map(per_block, (qs.reshape(nb, Q_BLOCK, nh, hd),
                                qis.reshape(nb, Q_BLOCK, IDX_HEADS, IDX_DIM),
                                ws.reshape(nb, Q_BLOCK, IDX_HEADS), pos_blk))
        return o.reshape(s_len, nh * hd)

    return lax.map(per_seq, (q, k, v, qi, w, ki))


def sparse_attention_step(q, k_new, v_new, qi, w, ki_new, k_cache, v_cache, ki_cache, topk):
    bsz, t, nh, hd = q.shape
    p_len = k_cache.shape[1]
    kk = jnp.concatenate([k_cache, k_new], 1)
    vv = jnp.concatenate([v_cache, v_new], 1)
    kki = jnp.concatenate([ki_cache, ki_new], 1)
    q_pos = p_len + jnp.arange(t)
    k_pos = jnp.arange(p_len + t)

    def per_seq(args):
        qs, ks, vs, qis, ws, kis = args
        return sparse_attend(qs, ks, vs, qis, ws, kis, q_pos, k_pos, topk)

    o = lax.map(per_seq, (q, kk, vv, qi, w, kki))
    return o.reshape(bsz, t, nh * hd)


def post_sublayers(x, oa, ob, w_out, ln1_g, ln1_b, w_up, w_down, ln2_g, ln2_b):
    mix = jnp.einsum('bte,ed->btd', jnp.concatenate([oa, ob], -1), w_out)
    x = layer_norm(DN_ALPHA * x + mix, ln1_g, ln1_b)
    hid = jnp.square(jax.nn.relu(jnp.einsum('btd,df->btf', x, w_up)))
    ff = jnp.einsum('btf,fd->btd', hid, w_down)
    return layer_norm(DN_ALPHA * x + ff, ln2_g, ln2_b)


def setup_inputs(seed: int = 0) -> dict:
    key = jax.random.key(seed)
    ks = jax.random.split(key, 20)
    f32 = jnp.float32
    a_keep = min(A_PAST, PAST_LEN)
    x_prompt = jax.random.normal(ks[0], (BATCH, SEQ, D_MODEL), f32)
    x_sample = jax.random.normal(ks[1], (DEC_BATCH, DEC_SEQ, D_MODEL), f32)
    cache_a_k = jax.random.normal(ks[2], (DEPTH, DEC_BATCH, a_keep, A_HEADS, HEAD_DIM), f32)
    cache_a_v = jax.random.normal(ks[3], (DEPTH, DEC_BATCH, a_keep, A_HEADS, HEAD_DIM), f32) * DN_BETA
    cache_b_k = jax.random.normal(ks[4], (DEPTH, DEC_BATCH, PAST_LEN, B_HEADS, HEAD_DIM), f32)
    cache_b_v = jax.random.normal(ks[5], (DEPTH, DEC_BATCH, PAST_LEN, B_HEADS, HEAD_DIM), f32) * DN_BETA
    cache_b_kidx = jax.random.normal(ks[6], (DEPTH, DEC_BATCH, PAST_LEN, IDX_DIM), f32)
    v_a0 = 2 * A_WIDTH
    v_b0 = 3 * A_WIDTH + 2 * B_WIDTH
    col_scale = jnp.ones((IN_WIDTH,), f32).at[v_a0:v_a0 + A_WIDTH].set(DN_BETA).at[v_b0:v_b0 + B_WIDTH].set(DN_BETA)
    w_in = jax.random.normal(ks[7], (DEPTH, D_MODEL, IN_WIDTH), f32) * D_MODEL ** -0.5 * col_scale
    rel_bias = jax.random.normal(ks[8], (DEPTH, A_HEADS, N_REL), f32) * 0.5
    w_out = jax.random.normal(ks[9], (DEPTH, MIX_WIDTH, D_MODEL), f32) * MIX_WIDTH ** -0.5 * DN_BETA
    ln1_g = 1.0 + 0.05 * jax.random.normal(ks[10], (DEPTH, D_MODEL), f32)
    ln1_b = 0.02 * jax.random.normal(ks[11], (DEPTH, D_MODEL), f32)
    w_up = jax.random.normal(ks[12], (DEPTH, D_MODEL, D_FF), f32) * D_MODEL ** -0.5 * DN_BETA
    w_down = jax.random.normal(ks[13], (DEPTH, D_FF, D_MODEL), f32) * D_FF ** -0.5 * DN_BETA
    ln2_g = 1.0 + 0.05 * jax.random.normal(ks[14], (DEPTH, D_MODEL), f32)
    ln2_b = 0.02 * jax.random.normal(ks[15], (DEPTH, D_MODEL), f32)
    return {'x_prompt': x_prompt, 'x_sample': x_sample,
            'cache_a_k': cache_a_k, 'cache_a_v': cache_a_v,
            'cache_b_k': cache_b_k, 'cache_b_v': cache_b_v, 'cache_b_kidx': cache_b_kidx,
            'w_in': w_in, 'rel_bias': rel_bias, 'w_out': w_out,
            'ln1_g': ln1_g, 'ln1_b': ln1_b, 'w_up': w_up, 'w_down': w_down,
            'ln2_g': ln2_g, 'ln2_b': ln2_b}


def reference(x_prompt, x_sample, cache_a_k, cache_a_v, cache_b_k, cache_b_v, cache_b_kidx,
              w_in, rel_bias, w_out, ln1_g, ln1_b, w_up, w_down, ln2_g, ln2_b):
    s_len = x_prompt.shape[1]
    t_len = x_sample.shape[1]
    p_len = cache_b_k.shape[2]
    pos_p = jnp.arange(s_len)
    pos_s = p_len + jnp.arange(t_len)
    topk_p = min(TOPK_MAX, s_len // 4)
    topk_s = min(TOPK_MAX, (p_len + t_len) // 4)
    keep_a = min(A_PAST, s_len)

    xp, xs = x_prompt, x_sample
    ak_p, av_p, bk_p, bv_p, bi_p = [], [], [], [], []
    ak_s, av_s, bk_s, bv_s, bi_s = [], [], [], [], []
    for l in range(DEPTH):
        qa, ka, va, qb, kb, vb, qi, ki, wi = project(xp, w_in[l], pos_p)
        oa = band_attention_prompt(qa, ka, va, rel_bias[l])
        ob = sparse_attention_prompt(qb, kb, vb, qi, wi, ki, pos_p, topk_p)
        xp = post_sublayers(xp, oa, ob, w_out[l], ln1_g[l], ln1_b[l], w_up[l], w_down[l], ln2_g[l], ln2_b[l])
        ak_p.append(ka[:, s_len - keep_a:])
        av_p.append(va[:, s_len - keep_a:])
        bk_p.append(kb)
        bv_p.append(vb)
        bi_p.append(ki)
        qa, ka, va, qb, kb, vb, qi, ki, wi = project(xs, w_in[l], pos_s)
        oa = band_attention_step(qa, ka, va, cache_a_k[l], cache_a_v[l], rel_bias[l])
        ob = sparse_attention_step(qb, kb, vb, qi, wi, ki, cache_b_k[l], cache_b_v[l], cache_b_kidx[l], topk_s)
        xs = post_sublayers(xs, oa, ob, w_out[l], ln1_g[l], ln1_b[l], w_up[l], w_down[l], ln2_g[l], ln2_b[l])
        ak_s.append(ka)
        av_s.append(va)
        bk_s.append(kb)
        bv_s.append(vb)
        bi_s.append(ki)

    return (xp, xs,
            jnp.stack(ak_p), jnp.stack(av_p), jnp.stack(bk_p), jnp.stack(bv_p), jnp.stack(bi_p),
            jnp.stack(ak_s), jnp.stack(av_s), jnp.stack(bk_s), jnp.stack(bv_s), jnp.stack(bi_s))
```

```python
import functools

import jax
import jax.numpy as jnp
import numpy as np
from jax import lax
from jax.experimental import pallas as pl
from jax.experimental.pallas import tpu as pltpu

F32 = jnp.float32
BF16 = jnp.bfloat16
I32 = jnp.int32

CHUNK = 64
HEAD_DIM = 64
N_HEADS = 8
GROUP_W = N_HEADS * HEAD_DIM
A_PAST = 8 * CHUNK
A_BAND = A_PAST + CHUNK
REL_CLIP = 128
IDX_HEADS = 8
IDX_DIM = 64
TOPK_MAX = 256
ROPE_THETA = 10000.0
LN_EPS = 1e-5

LANES = 128
HEADS_PER_SLAB = LANES // HEAD_DIM
N_SLABS = GROUP_W // LANES
ROW_TILE = 512
VMEM_LIMIT = 56 * 1024 * 1024

IN_MAIN = 7 * GROUP_W
IN_SMALL0 = IN_MAIN
IN_PAD = IN_MAIN + LANES

INT_MIN = np.int32(-2 ** 31)
KEY_NEG_INF = np.int32(np.int32(np.float32(-np.inf).view(np.int32)) ^ np.int32(0x7FFFFFFF))


def _cparams(n_axes):
    return pltpu.CompilerParams(dimension_semantics=("arbitrary",) * n_axes,
                                vmem_limit_bytes=VMEM_LIMIT)


def _proj_kernel(x_ref, w_ref, cos_ref, sin_ref,
                 qa_o, ka_o, va_o, qb_o, kb_o, vb_o, qi_o, ki2_o,
                 kaf_o, vaf_o, kbf_o, vbf_o, small_o, *, keep_tiles):
    i = pl.program_id(0)
    tm = x_ref.shape[0]
    x = x_ref[...].astype(BF16)
    cos = cos_ref[...]
    sin = sin_ref[...]
    lane = lax.broadcasted_iota(I32, (tm, LANES), 1)
    first_half = (lane & (HEAD_DIM - 1)) < (HEAD_DIM // 2)

    def sec(c0, n):
        return jnp.dot(x, w_ref[:, c0:c0 + n], preferred_element_type=F32)

    def rope_slab(slab):
        swapped = jnp.where(first_half, pltpu.roll(slab, LANES - HEAD_DIM // 2, 1),
                            pltpu.roll(slab, HEAD_DIM // 2, 1))
        return slab * cos + swapped * sin

    def rope(h):
        return jnp.concatenate([rope_slab(h[:, s * LANES:(s + 1) * LANES])
                                for s in range(h.shape[1] // LANES)], axis=1)

    last = (i % keep_tiles) == (keep_tiles - 1)

    qa_o[...] = (sec(0 * GROUP_W, GROUP_W) * (HEAD_DIM ** -0.5)).astype(BF16)
    ka = sec(1 * GROUP_W, GROUP_W)
    ka_o[...] = ka.astype(BF16)
    va = sec(2 * GROUP_W, GROUP_W)
    va_o[...] = va.astype(BF16)

    @pl.when(last)
    def _():
        kaf_o[...] = ka
        vaf_o[...] = va

    qb_o[...] = (rope(sec(3 * GROUP_W, GROUP_W)) * (HEAD_DIM ** -0.5)).astype(BF16)
    kb = rope(sec(4 * GROUP_W, GROUP_W))
    kbf_o[...] = kb
    kb_o[...] = kb.astype(BF16)
    vb = sec(5 * GROUP_W, GROUP_W)
    vbf_o[...] = vb
    vb_o[...] = vb.astype(BF16)
    qi_o[...] = rope(sec(6 * GROUP_W, GROUP_W)).astype(BF16)

    sm = sec(IN_SMALL0, LANES)
    ki = rope_slab(sm)
    is_ki = lane < IDX_DIM
    small_o[...] = jnp.where(is_ki, ki, sm * ((IDX_HEADS * IDX_DIM) ** -0.5))
    ki2_o[...] = jnp.where(is_ki, ki, pltpu.roll(ki, IDX_DIM, 1)).astype(BF16)


def _project(x2d, w, cos_t, sin_t, keep_tiles):
    R, D = x2d.shape
    tm = ROW_TILE
    n_tiles = R // tm
    n_pos = cos_t.shape[0] // tm
    row = lambda i: (i, 0)
    bf = lambda n: jax.ShapeDtypeStruct((R, n), BF16)
    ff = lambda r, n: jax.ShapeDtypeStruct((r, n), F32)
    blk = lambda n: pl.BlockSpec((tm, n), row)
    keep_blk = pl.BlockSpec((tm, GROUP_W), lambda i: (i // keep_tiles, 0))
    out_shape = [bf(GROUP_W)] * 7 + [bf(LANES)] + [
        ff(R // keep_tiles, GROUP_W), ff(R // keep_tiles, GROUP_W),
        ff(R, GROUP_W), ff(R, GROUP_W), ff(R, LANES)]
    out_specs = [blk(GROUP_W)] * 7 + [blk(LANES)] + [
        keep_blk, keep_blk, blk(GROUP_W), blk(GROUP_W), blk(LANES)]
    return pl.pallas_call(
        functools.partial(_proj_kernel, keep_tiles=keep_tiles),
        grid=(n_tiles,),
        in_specs=[pl.BlockSpec((tm, D), row),
                  pl.BlockSpec((D, IN_PAD), lambda i: (0, 0)),
                  pl.BlockSpec((tm, LANES), lambda i: (i % n_pos, 0)),
                  pl.BlockSpec((tm, LANES), lambda i: (i % n_pos, 0))],
        out_specs=out_specs,
        out_shape=out_shape,
        compiler_params=_cparams(1),
        name="proj_rope",
    )(x2d, w, cos_t, sin_t)


def _head_masks(rows):
    lane = lax.broadcasted_iota(I32, (rows, LANES), 1)
    return [jnp.where((lane >= h * HEAD_DIM) & (lane < (h + 1) * HEAD_DIM), 1.0, 0.0).astype(BF16)
            for h in range(HEADS_PER_SLAB)]


def _band_kernel(q_ref, k_ref, v_ref, bias_ref, o_ref, *, n_chunks, first_chunk):
    lane = lax.broadcasted_iota(I32, (CHUNK, LANES), 1)
    band_idx = lax.broadcasted_iota(I32, (CHUNK, A_BAND), 1)
    head_masks = _head_masks(CHUNK)

    def chunk(c, carry):
        qs = pl.multiple_of(c * CHUNK, CHUNK)
        q2 = q_ref[0, pl.ds(qs, CHUNK), :]
        kb = k_ref[0, pl.ds(qs, A_BAND), :]
        vb = v_ref[0, pl.ds(qs, A_BAND), :]
        n_missing = (A_PAST // CHUNK - (c + first_chunk)) * CHUNK
        outs = []
        for h in range(HEADS_PER_SLAB):
            qm = q2 * head_masks[h]
            s = lax.dot_general(qm, kb, (((1,), (1,)), ((), ())), preferred_element_type=F32)
            s = s + bias_ref[h]
            s = jnp.where(band_idx >= n_missing, s, -jnp.inf)
            m = jnp.max(s, axis=1, keepdims=True)
            p = jnp.exp(s - m)
            l = jnp.sum(p, axis=1, keepdims=True)
            o = jnp.dot(p.astype(BF16), vb, preferred_element_type=F32)
            outs.append(o / l)
        o2 = jnp.where(lane < HEAD_DIM, outs[0], outs[1])
        o_ref[0, pl.ds(qs, CHUNK), :] = o2.astype(BF16)
        return carry

    lax.fori_loop(0, n_chunks, chunk, 0)


def _band_attention(q, kpad, vpad, bias, first_chunk):
    B, S, _ = q.shape
    n_chunks = S // CHUNK
    return pl.pallas_call(
        functools.partial(_band_kernel, n_chunks=n_chunks, first_chunk=first_chunk),
        grid=(B, N_SLABS),
        in_specs=[pl.BlockSpec((1, S, LANES), lambda b, g: (b, 0, g)),
                  pl.BlockSpec((1, S + A_PAST, LANES), lambda b, g: (b, 0, g)),
                  pl.BlockSpec((1, S + A_PAST, LANES), lambda b, g: (b, 0, g)),
                  pl.BlockSpec((HEADS_PER_SLAB, CHUNK, A_BAND), lambda b, g: (g, 0, 0))],
        out_specs=pl.BlockSpec((1, S, LANES), lambda b, g: (b, 0, g)),
        out_shape=jax.ShapeDtypeStruct((B, S, GROUP_W), BF16),
        compiler_params=_cparams(2),
        name="band_attention",
    )(q, kpad, vpad, bias)


def _sparse_body(qb_ref, kb_ref, vb_ref, qi_ref, ki2_ref, small_ref, o_ref, *, width, q_pos0, topk):
    tq = qb_ref.shape[1]
    W = width
    lane = lax.broadcasted_iota(I32, (tq, LANES), 1)
    head_masks = _head_masks(tq)

    small = small_ref[0]
    qi = qi_ref[0]
    ki2 = ki2_ref[0, :W, :]
    stacked = []
    for g in range(N_SLABS):
        q2 = qi[:, g * LANES:(g + 1) * LANES]
        for h in range(HEADS_PER_SLAB):
            stacked.append(q2 * head_masks[h])
    qstack = jnp.concatenate(stacked, axis=0)
    dots = lax.dot_general(qstack, ki2, (((1,), (1,)), ((), ())), preferred_element_type=F32)
    score = jnp.zeros((tq, W), F32)
    for h in range(IDX_HEADS):
        wcol = small[:, IDX_DIM + h:IDX_DIM + h + 1]
        score = score + wcol * jnp.maximum(dots[h * tq:(h + 1) * tq, :], 0.0)

    row = lax.broadcasted_iota(I32, (tq, W), 0)
    col = lax.broadcasted_iota(I32, (tq, W), 1)
    chunk_shift = CHUNK.bit_length() - 1
    adm = (col >> chunk_shift) <= ((row + q_pos0) >> chunk_shift)
    score = jnp.where(adm, score + 0.0, -jnp.inf)

    bits = pltpu.bitcast(score, I32)
    key = bits ^ ((bits >> 31) & np.int32(0x7FFFFFFF))

    def count_ge(cand):
        return jnp.sum(jnp.where(key >= cand, 1.0, 0.0), axis=1, keepdims=True)

    kf = np.float32(topk)
    t0 = jnp.where(count_ge(jnp.zeros((tq, 1), I32)) >= kf, np.int32(0), INT_MIN)

    def step(it, t):
        cand = t | (np.int32(1) << (np.int32(30) - it))
        return jnp.where(count_ge(cand) >= kf, cand, t)

    thr = lax.fori_loop(0, 31, step, t0)

    gt = key > thr
    eq = key == thr
    n_gt = jnp.sum(jnp.where(gt, 1.0, 0.0), axis=1, keepdims=True)
    n_eq = jnp.sum(jnp.where(eq, 1.0, 0.0), axis=1, keepdims=True)
    need = kf - n_gt
    excess = (n_eq > need) & (thr > KEY_NEG_INF)
    sel_simple = jnp.where((gt | eq) & adm, 0.0, -jnp.inf)

    def tie_path():
        eqf = jnp.where(eq, 1.0, 0.0).astype(BF16)
        ui = lax.broadcasted_iota(I32, (LANES, LANES), 0)
        uj = lax.broadcasted_iota(I32, (LANES, LANES), 1)
        upper = jnp.where(ui < uj, 1.0, 0.0).astype(BF16)
        ones = jnp.ones((LANES, LANES), BF16)
        offset = jnp.zeros((tq, 1), F32)
        pieces = []
        for t in range(W // LANES):
            e = eqf[:, t * LANES:(t + 1) * LANES]
            rank = offset + jnp.dot(e, upper, preferred_element_type=F32)
            pieces.append(rank < need)
            offset = offset + jnp.dot(e, ones, preferred_element_type=F32)[:, :1]
        tie_ok = jnp.concatenate(pieces, axis=1)
        return jnp.where((gt | (eq & tie_ok)) & adm, 0.0, -jnp.inf)

    any_excess = jnp.max(jnp.where(excess, 1.0, 0.0)) > 0.0
    negmask = lax.cond(any_excess, tie_path, lambda: sel_simple)

    qb = qb_ref[0]
    for g in range(N_SLABS):
        q2 = qb[:, g * LANES:(g + 1) * LANES]
        k2 = kb_ref[0, :W, g * LANES:(g + 1) * LANES]
        v2 = vb_ref[0, :W, g * LANES:(g + 1) * LANES]
        outs = []
        for h in range(HEADS_PER_SLAB):
            qm = q2 * head_masks[h]
            s = lax.dot_general(qm, k2, (((1,), (1,)), ((), ())), preferred_element_type=F32)
            s = s + negmask
            m = jnp.max(s, axis=1, keepdims=True)
            p = jnp.exp(s - m)
            l = jnp.sum(p, axis=1, keepdims=True)
            o = jnp.dot(p.astype(BF16), v2, preferred_element_type=F32)
            outs.append(o / l)
        o2 = jnp.where(lane < HEAD_DIM, outs[0], outs[1])
        o_ref[0, :, g * LANES:(g + 1) * LANES] = o2.astype(BF16)


def _sparse_kernel(qb_ref, kb_ref, vb_ref, qi_ref, ki2_ref, small_ref, o_ref, *,
                   widths, causal_blocks, q_pos_base, topk):
    tq = qb_ref.shape[1]
    j = pl.program_id(1)
    if not causal_blocks:
        _sparse_body(qb_ref, kb_ref, vb_ref, qi_ref, ki2_ref, small_ref, o_ref,
                     width=widths[0], q_pos0=q_pos_base, topk=topk)
        return
    q_pos0 = j * tq
    lo = 0
    for W in widths:
        @pl.when(((j + 1) * tq > lo) & ((j + 1) * tq <= W))
        def _(W=W):
            _sparse_body(qb_ref, kb_ref, vb_ref, qi_ref, ki2_ref, small_ref, o_ref,
                         width=W, q_pos0=q_pos0, topk=topk)
        lo = W


def _sparse_attention(qb, kb, vb, qi, ki2, small, *, tq, widths, causal_blocks, q_pos_base, topk):
    B, Sq, _ = qb.shape
    Lk = kb.shape[1]
    nq = Sq // tq
    qspec = lambda n: pl.BlockSpec((1, tq, n), lambda b, j: (b, j, 0))
    kspec = lambda n: pl.BlockSpec((1, Lk, n), lambda b, j: (b, 0, 0))
    return pl.pallas_call(
        functools.partial(_sparse_kernel, widths=widths, causal_blocks=causal_blocks,
                          q_pos_base=q_pos_base, topk=topk),
        grid=(B, nq),
        in_specs=[qspec(GROUP_W), kspec(GROUP_W), kspec(GROUP_W), qspec(GROUP_W), kspec(LANES), qspec(LANES)],
        out_specs=qspec(GROUP_W),
        out_shape=jax.ShapeDtypeStruct((B, Sq, GROUP_W), BF16),
        compiler_params=_cparams(2),
        name="sparse_attention",
    )(qb, kb, vb, qi, ki2, small)


def _layer_norm(x, g, b):
    mu = jnp.mean(x, axis=1, keepdims=True)
    xc = x - mu
    var = jnp.mean(xc * xc, axis=1, keepdims=True)
    return xc * lax.rsqrt(var + LN_EPS) * g + b


def _post_kernel(x_ref, oa_ref, ob_ref, wo_ref, g1_ref, b1_ref, wu_ref, wd_ref, g2_ref, b2_ref, y_ref,
                 *, alpha, ff_chunk):
    x = x_ref[...]
    mix = jnp.dot(oa_ref[...], wo_ref[:GROUP_W, :], preferred_element_type=F32)
    mix = mix + jnp.dot(ob_ref[...], wo_ref[GROUP_W:, :], preferred_element_type=F32)
    x1 = _layer_norm(alpha * x + mix, g1_ref[...], b1_ref[...])
    x1b = x1.astype(BF16)
    d_ff = wu_ref.shape[1]
    ff = jnp.zeros_like(x1)
    for c in range(d_ff // ff_chunk):
        hid = jnp.dot(x1b, wu_ref[:, c * ff_chunk:(c + 1) * ff_chunk], preferred_element_type=F32)
        hid = jnp.maximum(hid, 0.0)
        hid = (hid * hid).astype(BF16)
        ff = ff + jnp.dot(hid, wd_ref[c * ff_chunk:(c + 1) * ff_chunk, :], preferred_element_type=F32)
    y_ref[...] = _layer_norm(alpha * x1 + ff, g2_ref[...], b2_ref[...])


def _post(x2d, oa, ob, wo, g1, b1, wu, wd, g2, b2, alpha):
    R, D = x2d.shape
    tm = ROW_TILE
    d_ff = wu.shape[1]
    row = lambda i: (i, 0)
    const = lambda i: (0, 0)
    return pl.pallas_call(
        functools.partial(_post_kernel, alpha=alpha, ff_chunk=1024),
        grid=(R // tm,),
        in_specs=[pl.BlockSpec((tm, D), row),
                  pl.BlockSpec((tm, GROUP_W), row),
                  pl.BlockSpec((tm, GROUP_W), row),
                  pl.BlockSpec((2 * GROUP_W, D), const),
                  pl.BlockSpec((1, D), const), pl.BlockSpec((1, D), const),
                  pl.BlockSpec((D, d_ff), const),
                  pl.BlockSpec((d_ff, D), const),
                  pl.BlockSpec((1, D), const), pl.BlockSpec((1, D), const)],
        out_specs=pl.BlockSpec((tm, D), row),
        out_shape=jax.ShapeDtypeStruct((R, D), F32),
        compiler_params=_cparams(1),
        name="post_sublayers",
    )(x2d, oa, ob, wo, g1, b1, wu, wd, g2, b2)


def _rope_tables(pos):
    half = HEAD_DIM // 2
    inv = ROPE_THETA ** (-jnp.arange(half, dtype=F32) / half)
    ang = pos.astype(F32)[:, None] * inv[None, :]
    cos, sin = jnp.cos(ang), jnp.sin(ang)
    cos_h = jnp.concatenate([cos, cos], axis=1)
    sin_h = jnp.concatenate([-sin, sin], axis=1)
    return jnp.tile(cos_h, (1, HEADS_PER_SLAB)), jnp.tile(sin_h, (1, HEADS_PER_SLAB))


def _band_bias(table):
    dist = jnp.arange(CHUNK)[:, None] + A_PAST - jnp.arange(A_BAND)[None, :]
    idx = jnp.clip(dist, -REL_CLIP, REL_CLIP) + REL_CLIP
    return table[:, idx].astype(F32)


def _dup_lanes(ki):
    return jnp.concatenate([ki, ki], axis=-1)


def kernel(x_prompt, x_sample, cache_a_k, cache_a_v, cache_b_k, cache_b_v, cache_b_kidx,
           w_in, rel_bias, w_out, ln1_g, ln1_b, w_up, w_down, ln2_g, ln2_b):
    B, S, D = x_prompt.shape
    Bs, T, _ = x_sample.shape
    depth = w_in.shape[0]
    p_len = cache_b_k.shape[2]
    a_len = cache_a_k.shape[2]
    assert S % ROW_TILE == 0 and (Bs * T) % ROW_TILE == 0 and T == CHUNK and a_len == A_PAST
    alpha = float((2 * depth) ** 0.25)
    topk_p = min(TOPK_MAX, S // 4)
    topk_s = min(TOPK_MAX, (p_len + T) // 4)
    keep_a = min(A_PAST, S)
    assert keep_a == ROW_TILE
    tiles_per_seq = S // ROW_TILE

    cos_p, sin_p = _rope_tables(jnp.arange(S))
    pos_s = jnp.tile(p_len + jnp.arange(T), ROW_TILE // T)
    cos_s, sin_s = _rope_tables(pos_s)

    tq_p = 2 * CHUNK
    widths_p = tuple(range(ROW_TILE, S + 1, ROW_TILE))
    lk_s = -(-(p_len + T) // LANES) * LANES
    pad_s = lk_s - (p_len + T)

    xp = x_prompt.reshape(B * S, D)
    xs = x_sample.reshape(Bs * T, D)
    outs = {k: [] for k in ("ak_p", "av_p", "bk_p", "bv_p", "bi_p", "ak_s", "av_s", "bk_s", "bv_s", "bi_s")}

    for l in range(depth):
        w = jnp.pad(w_in[l], ((0, 0), (0, IN_PAD - w_in.shape[2]))).astype(BF16)
        wo = w_out[l].astype(BF16)
        wu = w_up[l].astype(BF16)
        wd = w_down[l].astype(BF16)
        g1, b1 = ln1_g[l][None, :], ln1_b[l][None, :]
        g2, b2 = ln2_g[l][None, :], ln2_b[l][None, :]
        bias = _band_bias(rel_bias[l])

        (qa, ka, va, qb, kb, vb, qi, ki2, kaf, vaf, kbf, vbf, small) = _project(xp, w, cos_p, sin_p, tiles_per_seq)
        r3 = lambda a: a.reshape(B, S, a.shape[-1])
        padk = lambda a: jnp.pad(r3(a), ((0, 0), (A_PAST, 0), (0, 0)))
        oa = _band_attention(r3(qa), padk(ka), padk(va), bias, first_chunk=0)
        ob = _sparse_attention(r3(qb), r3(kb), r3(vb), r3(qi), r3(ki2), r3(small), tq=tq_p,
                               widths=widths_p, causal_blocks=True, q_pos_base=0, topk=topk_p)
        xp = _post(xp, oa.reshape(B * S, GROUP_W), ob.reshape(B * S, GROUP_W), wo, g1, b1, wu, wd, g2, b2, alpha)
        outs["ak_p"].append(kaf.reshape(B, keep_a, N_HEADS, HEAD_DIM))
        outs["av_p"].append(vaf.reshape(B, keep_a, N_HEADS, HEAD_DIM))
        outs["bk_p"].append(kbf.reshape(B, S, N_HEADS, HEAD_DIM))
        outs["bv_p"].append(vbf.reshape(B, S, N_HEADS, HEAD_DIM))
        outs["bi_p"].append(small[:, :IDX_DIM].reshape(B, S, IDX_DIM))

        (qa, ka, va, qb, kb, vb, qi, ki2, kaf, vaf, kbf, vbf, small) = _project(xs, w, cos_s, sin_s, 1)
        s3 = lambda a: a.reshape(Bs, T, a.shape[-1])
        cat_a = lambda c, n: jnp.concatenate([c.reshape(Bs, a_len, GROUP_W).astype(BF16), s3(n)], axis=1)
        oa = _band_attention(s3(qa), cat_a(cache_a_k[l], ka), cat_a(cache_a_v[l], va), bias,
                             first_chunk=A_PAST // CHUNK)
        zpad = lambda n: jnp.zeros((Bs, pad_s, n), BF16)
        cat_b = lambda c, n: jnp.concatenate(
            [c.reshape(Bs, p_len, GROUP_W).astype(BF16), s3(n), zpad(GROUP_W)], axis=1)
        ki2_all = jnp.concatenate([_dup_lanes(cache_b_kidx[l]).astype(BF16), s3(ki2), zpad(LANES)], axis=1)
        ob = _sparse_attention(s3(qb), cat_b(cache_b_k[l], kb), cat_b(cache_b_v[l], vb), s3(qi), ki2_all,
                               s3(small), tq=T, widths=(lk_s,), causal_blocks=False, q_pos_base=p_len,
                               topk=topk_s)
        xs = _post(xs, oa.reshape(Bs * T, GROUP_W), ob.reshape(Bs * T, GROUP_W), wo, g1, b1, wu, wd, g2, b2, alpha)
        outs["ak_s"].append(kaf.reshape(Bs, T, N_HEADS, HEAD_DIM))
        outs["av_s"].append(vaf.reshape(Bs, T, N_HEADS, HEAD_DIM))
        outs["bk_s"].append(kbf.reshape(Bs, T, N_HEADS, HEAD_DIM))
        outs["bv_s"].append(vbf.reshape(Bs, T, N_HEADS, HEAD_DIM))
        outs["bi_s"].append(small[:, :IDX_DIM].reshape(Bs, T, IDX_DIM))

    st = lambda k: jnp.stack(outs[k])
    return (xp.reshape(B, S, D), xs.reshape(Bs, T, D),
            st("ak_p"), st("av_p"), st("bk_p"), st("bv_p"), st("bi_p"),
            st("ak_s"), st("av_s"), st("bk_s"), st("bv_s"), st("bi_s"))
```

```python
import functools

import jax
import jax.numpy as jnp
import numpy as np
from jax import lax
from jax.experimental import pallas as pl
from jax.experimental.pallas import tpu as pltpu

F32 = jnp.float32
BF16 = jnp.bfloat16
I32 = jnp.int32
I16 = jnp.int16

CHUNK = 64
HEAD_DIM = 64
N_HEADS = 8
GROUP_W = N_HEADS * HEAD_DIM
A_PAST = 8 * CHUNK
A_BAND = A_PAST + CHUNK
REL_CLIP = 128
IDX_HEADS = 8
IDX_DIM = 64
TOPK_MAX = 256
ROPE_THETA = 10000.0
LN_EPS = 1e-5

LANES = 128
HEADS_PER_SLAB = LANES // HEAD_DIM
N_SLABS = GROUP_W // LANES
ROW_TILE = 512
BAND_GROUP_CHUNKS = 4
SEARCH_BITS = 16
VMEM_LIMIT = 56 * 1024 * 1024

IN_MAIN = 7 * GROUP_W
IN_SMALL0 = IN_MAIN
IN_PAD = IN_MAIN + LANES

KEY_NEG_INF = np.int32(np.int32(np.float32(-np.inf).view(np.int32)) ^ np.int32(0x7FFFFFFF))


def _cparams(n_axes):
    return pltpu.CompilerParams(dimension_semantics=("arbitrary",) * n_axes,
                                vmem_limit_bytes=VMEM_LIMIT)


def _proj_kernel(x_ref, w_ref, cos_ref, sin_ref,
                 qa_o, ka_o, va_o, qb_o, kb_o, vb_o, qi_o, ki2_o,
                 kaf_o, vaf_o, kbf_o, vbf_o, small_o, *, keep_tiles):
    i = pl.program_id(0)
    tm = x_ref.shape[0]
    x = x_ref[...].astype(BF16)
    cos = cos_ref[...]
    sin = sin_ref[...]
    lane = lax.broadcasted_iota(I32, (tm, LANES), 1)
    first_half = (lane & (HEAD_DIM - 1)) < (HEAD_DIM // 2)

    def sec(c0, n):
        return jnp.dot(x, w_ref[:, c0:c0 + n], preferred_element_type=F32)

    def rope_slab(slab):
        swapped = jnp.where(first_half, pltpu.roll(slab, LANES - HEAD_DIM // 2, 1),
                            pltpu.roll(slab, HEAD_DIM // 2, 1))
        return slab * cos + swapped * sin

    def rope(h):
        return jnp.concatenate([rope_slab(h[:, s * LANES:(s + 1) * LANES])
                                for s in range(h.shape[1] // LANES)], axis=1)

    last = (i % keep_tiles) == (keep_tiles - 1)

    qa_o[...] = (sec(0 * GROUP_W, GROUP_W) * (HEAD_DIM ** -0.5)).astype(BF16)
    ka = sec(1 * GROUP_W, GROUP_W)
    ka_o[...] = ka.astype(BF16)
    va = sec(2 * GROUP_W, GROUP_W)
    va_o[...] = va.astype(BF16)

    @pl.when(last)
    def _():
        kaf_o[...] = ka
        vaf_o[...] = va

    qb_o[...] = (rope(sec(3 * GROUP_W, GROUP_W)) * (HEAD_DIM ** -0.5)).astype(BF16)
    kb = rope(sec(4 * GROUP_W, GROUP_W))
    kbf_o[...] = kb
    kb_o[...] = kb.astype(BF16)
    vb = sec(5 * GROUP_W, GROUP_W)
    vbf_o[...] = vb
    vb_o[...] = vb.astype(BF16)
    qi_o[...] = rope(sec(6 * GROUP_W, GROUP_W)).astype(BF16)

    sm = sec(IN_SMALL0, LANES)
    ki = rope_slab(sm)
    is_ki = lane < IDX_DIM
    small_o[...] = jnp.where(is_ki, ki, sm * ((IDX_HEADS * IDX_DIM) ** -0.5))
    ki2_o[...] = jnp.where(is_ki, ki, pltpu.roll(ki, IDX_DIM, 1)).astype(BF16)


def _project(x2d, w, cos_t, sin_t, keep_tiles):
    R, D = x2d.shape
    tm = ROW_TILE
    n_tiles = R // tm
    n_pos = cos_t.shape[0] // tm
    row = lambda i: (i, 0)
    bf = lambda n: jax.ShapeDtypeStruct((R, n), BF16)
    ff = lambda r, n: jax.ShapeDtypeStruct((r, n), F32)
    blk = lambda n: pl.BlockSpec((tm, n), row)
    keep_blk = pl.BlockSpec((tm, GROUP_W), lambda i: (i // keep_tiles, 0))
    out_shape = [bf(GROUP_W)] * 7 + [bf(LANES)] + [
        ff(R // keep_tiles, GROUP_W), ff(R // keep_tiles, GROUP_W),
        ff(R, GROUP_W), ff(R, GROUP_W), ff(R, LANES)]
    out_specs = [blk(GROUP_W)] * 7 + [blk(LANES)] + [
        keep_blk, keep_blk, blk(GROUP_W), blk(GROUP_W), blk(LANES)]
    return pl.pallas_call(
        functools.partial(_proj_kernel, keep_tiles=keep_tiles),
        grid=(n_tiles,),
        in_specs=[pl.BlockSpec((tm, D), row),
                  pl.BlockSpec((D, IN_PAD), lambda i: (0, 0)),
                  pl.BlockSpec((tm, LANES), lambda i: (i % n_pos, 0)),
                  pl.BlockSpec((tm, LANES), lambda i: (i % n_pos, 0))],
        out_specs=out_specs,
        out_shape=out_shape,
        compiler_params=_cparams(1),
        name="proj_rope",
    )(x2d, w, cos_t, sin_t)


def _head_masks(rows):
    lane = lax.broadcasted_iota(I32, (rows, LANES), 1)
    return [jnp.where((lane >= h * HEAD_DIM) & (lane < (h + 1) * HEAD_DIM), 1.0, 0.0).astype(BF16)
            for h in range(HEADS_PER_SLAB)]


def _band_kernel(q_ref, k_ref, v_ref, bias_ref, o_ref, *, n_groups, group_chunks, q_off):
    gq = group_chunks * CHUNK
    win = A_PAST + gq
    lane = lax.broadcasted_iota(I32, (gq, LANES), 1)
    head_masks = _head_masks(gq)
    for gi in range(n_groups):
        qs = gi * gq
        start = q_off + qs - A_PAST
        lo = max(start, 0)
        hi = q_off + qs + gq
        q2 = q_ref[0, qs:qs + gq, :]
        kw = k_ref[0, lo:hi, :]
        vw = v_ref[0, lo:hi, :]
        qst = jnp.concatenate([q2 * head_masks[h] for h in range(HEADS_PER_SLAB)], axis=0)
        s = lax.dot_general(qst, kw, (((1,), (1,)), ((), ())), preferred_element_type=F32)
        s = s + bias_ref[0, :, lo - start:win]
        m = jnp.max(s, axis=1, keepdims=True)
        p = jnp.exp(s - m)
        l = jnp.sum(p, axis=1, keepdims=True)
        o = jnp.dot(p.astype(BF16), vw, preferred_element_type=F32) / l
        o2 = jnp.where(lane < HEAD_DIM, o[:gq], o[gq:])
        o_ref[0, qs:qs + gq, :] = o2.astype(BF16)


def _band_attention(q, k, v, bias, q_off, group_chunks):
    B, Sq, _ = q.shape
    Sk = k.shape[1]
    gq = group_chunks * CHUNK
    return pl.pallas_call(
        functools.partial(_band_kernel, n_groups=Sq // gq, group_chunks=group_chunks, q_off=q_off),
        grid=(B, N_SLABS),
        in_specs=[pl.BlockSpec((1, Sq, LANES), lambda b, g: (b, 0, g)),
                  pl.BlockSpec((1, Sk, LANES), lambda b, g: (b, 0, g)),
                  pl.BlockSpec((1, Sk, LANES), lambda b, g: (b, 0, g)),
                  pl.BlockSpec((1, HEADS_PER_SLAB * gq, A_PAST + gq), lambda b, g: (g, 0, 0))],
        out_specs=pl.BlockSpec((1, Sq, LANES), lambda b, g: (b, 0, g)),
        out_shape=jax.ShapeDtypeStruct((B, Sq, GROUP_W), BF16),
        compiler_params=_cparams(2),
        name="band_attention",
    )(q, k, v, bias)


def _count16(vals, cand, strict):
    rows, width = vals.shape
    c16 = jnp.broadcast_to(cand, (rows, LANES)).astype(I16)
    hits = []
    for t in range(width // LANES):
        v = vals[:, t * LANES:(t + 1) * LANES]
        hits.append(jnp.where((v > c16) if strict else (v >= c16), np.int16(1), np.int16(0)))
    while len(hits) > 1:
        hits = [a + b for a, b in zip(hits[::2], hits[1::2])] + ([hits[-1]] if len(hits) % 2 else [])
    return jnp.sum(hits[0].astype(F32), axis=1, keepdims=True)


def _kth_largest16(vals, need):
    rows = vals.shape[0]
    bias = np.int32(2 ** 15)
    u = jnp.zeros((rows, 1), I32)
    for pair in range(SEARCH_BITS // 2 - 1, -1, -1):
        b1, b0 = np.int32(2 << (2 * pair)), np.int32(1 << (2 * pair))
        ok = {c: _count16(vals, u + np.int32(c) - bias, False) >= need for c in (b1, b1 + b0, b0)}
        u = jnp.where(ok[b1], jnp.where(ok[b1 + b0], u + (b1 + b0), u + b1), jnp.where(ok[b0], u + b0, u))
    return u - bias


def _kth_largest_key(key, kf):
    rows, width = key.shape
    hi = (key >> 16).astype(I16)
    lo = ((key & np.int32(0xFFFF)) - np.int32(2 ** 15)).astype(I16)
    p_hi = _kth_largest16(hi, kf)
    need_lo = kf - _count16(hi, p_hi, True)
    p16 = jnp.broadcast_to(p_hi, (rows, LANES)).astype(I16)
    lo_in_bucket = jnp.concatenate(
        [jnp.where(hi[:, t * LANES:(t + 1) * LANES] == p16, lo[:, t * LANES:(t + 1) * LANES], np.int16(-2 ** 15))
         for t in range(width // LANES)], axis=1)
    p_lo = _kth_largest16(lo_in_bucket, need_lo)
    return (p_hi << 16) + (p_lo + np.int32(2 ** 15))


def _admissible(tq, W, q_pos0):
    row = lax.broadcasted_iota(I32, (tq, W), 0)
    col = lax.broadcasted_iota(I32, (tq, W), 1)
    chunk_shift = CHUNK.bit_length() - 1
    return (col >> chunk_shift) <= ((row + q_pos0) >> chunk_shift)


def _topk_negmask(qi_ref, ki2_ref, small_ref, *, width, q_pos0, topk):
    tq = qi_ref.shape[1]
    W = width
    head_masks = _head_masks(tq)

    small = small_ref[0]
    qi = qi_ref[0]
    ki2 = ki2_ref[0, :W, :]
    stacked = []
    for g in range(N_SLABS):
        q2 = qi[:, g * LANES:(g + 1) * LANES]
        for h in range(HEADS_PER_SLAB):
            stacked.append(q2 * head_masks[h])
    qstack = jnp.concatenate(stacked, axis=0)
    dots = lax.dot_general(qstack, ki2, (((1,), (1,)), ((), ())), preferred_element_type=F32)
    score = jnp.zeros((tq, W), F32)
    for h in range(IDX_HEADS):
        wcol = small[:, IDX_DIM + h:IDX_DIM + h + 1]
        score = score + wcol * jnp.maximum(dots[h * tq:(h + 1) * tq, :], 0.0)

    adm = _admissible(tq, W, q_pos0)
    score = jnp.where(adm, score + 0.0, -jnp.inf)

    bits = pltpu.bitcast(score, I32)
    key = bits ^ ((bits >> 31) & np.int32(0x7FFFFFFF))
    kf = np.float32(topk)
    thr = _kth_largest_key(key, kf)

    gt = key > thr
    eq = key == thr
    n_gt = jnp.sum(jnp.where(gt, 1.0, 0.0), axis=1, keepdims=True)
    n_eq = jnp.sum(jnp.where(eq, 1.0, 0.0), axis=1, keepdims=True)
    need = kf - n_gt
    excess = (n_eq > need) & (thr > KEY_NEG_INF)
    sel_simple = jnp.where((gt | eq) & adm, 0.0, -jnp.inf)

    def tie_path():
        eqf = jnp.where(eq, 1.0, 0.0).astype(BF16)
        ui = lax.broadcasted_iota(I32, (LANES, LANES), 0)
        uj = lax.broadcasted_iota(I32, (LANES, LANES), 1)
        upper = jnp.where(ui < uj, 1.0, 0.0).astype(BF16)
        ones = jnp.ones((LANES, LANES), BF16)
        offset = jnp.zeros((tq, 1), F32)
        pieces = []
        for t in range(W // LANES):
            e = eqf[:, t * LANES:(t + 1) * LANES]
            rank = offset + jnp.dot(e, upper, preferred_element_type=F32)
            pieces.append(rank < need)
            offset = offset + jnp.dot(e, ones, preferred_element_type=F32)[:, :1]
        tie_ok = jnp.concatenate(pieces, axis=1)
        return jnp.where((gt | (eq & tie_ok)) & adm, 0.0, -jnp.inf)

    any_excess = jnp.max(jnp.where(excess, 1.0, 0.0)) > 0.0
    return lax.cond(any_excess, tie_path, lambda: sel_simple)


def _masked_attention(qb_ref, kb_ref, vb_ref, o_ref, negmask, *, width):
    tq = qb_ref.shape[1]
    W = width
    lane = lax.broadcasted_iota(I32, (tq, LANES), 1)
    head_masks = _head_masks(tq)
    qb = qb_ref[0]
    for g in range(N_SLABS):
        q2 = qb[:, g * LANES:(g + 1) * LANES]
        k2 = kb_ref[0, :W, g * LANES:(g + 1) * LANES]
        v2 = vb_ref[0, :W, g * LANES:(g + 1) * LANES]
        outs = []
        for h in range(HEADS_PER_SLAB):
            qm = q2 * head_masks[h]
            s = lax.dot_general(qm, k2, (((1,), (1,)), ((), ())), preferred_element_type=F32)
            s = s + negmask
            m = jnp.max(s, axis=1, keepdims=True)
            p = jnp.exp(s - m)
            l = jnp.sum(p, axis=1, keepdims=True)
            o = jnp.dot(p.astype(BF16), v2, preferred_element_type=F32)
            outs.append(o / l)
        o2 = jnp.where(lane < HEAD_DIM, outs[0], outs[1])
        o_ref[0, :, g * LANES:(g + 1) * LANES] = o2.astype(BF16)


def _sparse_kernel(qb_ref, kb_ref, vb_ref, qi_ref, ki2_ref, small_ref, o_ref, *,
                   widths, causal_blocks, q_pos_base, topk):
    tq = qb_ref.shape[1]
    j = pl.program_id(1)

    def body(W, q_pos0):
        negmask = _topk_negmask(qi_ref, ki2_ref, small_ref, width=W, q_pos0=q_pos0, topk=topk)
        _masked_attention(qb_ref, kb_ref, vb_ref, o_ref, negmask, width=W)

    if not causal_blocks:
        body(widths[0], q_pos_base)
        return
    q_pos0 = j * tq
    n_keys = (j + 1) * tq
    dense_w = -(-topk // LANES) * LANES

    @pl.when(n_keys <= topk)
    def _():
        negmask = jnp.where(_admissible(tq, dense_w, q_pos0), 0.0, -jnp.inf)
        _masked_attention(qb_ref, kb_ref, vb_ref, o_ref, negmask, width=dense_w)

    lo = topk
    for W in widths:
        @pl.when((n_keys > lo) & (n_keys <= W))
        def _(W=W):
            body(W, q_pos0)
        lo = max(W, topk)


def _sparse_attention(qb, kb, vb, qi, ki2, small, *, tq, widths, causal_blocks, q_pos_base, topk):
    B, Sq, _ = qb.shape
    Lk = kb.shape[1]
    nq = Sq // tq
    qspec = lambda n: pl.BlockSpec((1, tq, n), lambda b, j: (b, j, 0))
    kspec = lambda n: pl.BlockSpec((1, Lk, n), lambda b, j: (b, 0, 0))
    return pl.pallas_call(
        functools.partial(_sparse_kernel, widths=widths, causal_blocks=causal_blocks,
                          q_pos_base=q_pos_base, topk=topk),
        grid=(B, nq),
        in_specs=[qspec(GROUP_W), kspec(GROUP_W), kspec(GROUP_W), qspec(GROUP_W), kspec(LANES), qspec(LANES)],
        out_specs=qspec(GROUP_W),
        out_shape=jax.ShapeDtypeStruct((B, Sq, GROUP_W), BF16),
        compiler_params=_cparams(2),
        name="sparse_attention",
    )(qb, kb, vb, qi, ki2, small)


def _layer_norm(x, g, b):
    mu = jnp.mean(x, axis=1, keepdims=True)
    xc = x - mu
    var = jnp.mean(xc * xc, axis=1, keepdims=True)
    return xc * lax.rsqrt(var + LN_EPS) * g + b


def _post_kernel(x_ref, oa_ref, ob_ref, wo_ref, g1_ref, b1_ref, wu_ref, wd_ref, g2_ref, b2_ref, y_ref,
                 *, alpha, ff_chunk):
    x = x_ref[...]
    mix = jnp.dot(oa_ref[...], wo_ref[:GROUP_W, :], preferred_element_type=F32)
    mix = mix + jnp.dot(ob_ref[...], wo_ref[GROUP_W:, :], preferred_element_type=F32)
    x1 = _layer_norm(alpha * x + mix, g1_ref[...], b1_ref[...])
    x1b = x1.astype(BF16)
    d_ff = wu_ref.shape[1]
    ff = jnp.zeros_like(x1)
    for c in range(d_ff // ff_chunk):
        hid = jnp.dot(x1b, wu_ref[:, c * ff_chunk:(c + 1) * ff_chunk], preferred_element_type=F32)
        hid = jnp.maximum(hid, 0.0)
        hid = (hid * hid).astype(BF16)
        ff = ff + jnp.dot(hid, wd_ref[c * ff_chunk:(c + 1) * ff_chunk, :], preferred_element_type=F32)
    y_ref[...] = _layer_norm(alpha * x1 + ff, g2_ref[...], b2_ref[...])


def _post(x2d, oa, ob, wo, g1, b1, wu, wd, g2, b2, alpha):
    R, D = x2d.shape
    tm = ROW_TILE
    d_ff = wu.shape[1]
    row = lambda i: (i, 0)
    const = lambda i: (0, 0)
    return pl.pallas_call(
        functools.partial(_post_kernel, alpha=alpha, ff_chunk=1024),
        grid=(R // tm,),
        in_specs=[pl.BlockSpec((tm, D), row),
                  pl.BlockSpec((tm, GROUP_W), row),
                  pl.BlockSpec((tm, GROUP_W), row),
                  pl.BlockSpec((2 * GROUP_W, D), const),
                  pl.BlockSpec((1, D), const), pl.BlockSpec((1, D), const),
                  pl.BlockSpec((D, d_ff), const),
                  pl.BlockSpec((d_ff, D), const),
                  pl.BlockSpec((1, D), const), pl.BlockSpec((1, D), const)],
        out_specs=pl.BlockSpec((tm, D), row),
        out_shape=jax.ShapeDtypeStruct((R, D), F32),
        compiler_params=_cparams(1),
        name="post_sublayers",
    )(x2d, oa, ob, wo, g1, b1, wu, wd, g2, b2)


def _rope_tables(pos):
    half = HEAD_DIM // 2
    inv = ROPE_THETA ** (-jnp.arange(half, dtype=F32) / half)
    ang = pos.astype(F32)[:, None] * inv[None, :]
    cos, sin = jnp.cos(ang), jnp.sin(ang)
    cos_h = jnp.concatenate([cos, cos], axis=1)
    sin_h = jnp.concatenate([-sin, sin], axis=1)
    return jnp.tile(cos_h, (1, HEADS_PER_SLAB)), jnp.tile(sin_h, (1, HEADS_PER_SLAB))


def _band_bias(table, group_chunks):
    gq = group_chunks * CHUNK
    win = A_PAST + gq
    d_min, d_max = A_PAST - (win - 1), gq - 1 + A_PAST
    n = d_max - d_min + 1
    left = jnp.repeat(table[:, :1], max(0, -REL_CLIP - d_min), axis=1)
    mid = table[:, max(d_min, -REL_CLIP) + REL_CLIP:min(d_max, REL_CLIP) + REL_CLIP + 1]
    right = jnp.repeat(table[:, -1:], max(0, d_max - REL_CLIP), axis=1)
    ext = jnp.concatenate([left, mid, right], axis=1).astype(F32)
    rp = jnp.concatenate([ext[:, ::-1], jnp.zeros((N_HEADS, 1), F32)], axis=1)
    shifted = jnp.tile(rp, (1, gq))[:, :gq * n].reshape(N_HEADS, gq, n)
    bias = shifted[:, :, gq - 1:gq - 1 + win]
    q_sub = jnp.arange(gq)[:, None] // CHUNK
    k_rel = jnp.arange(win)[None, :] - q_sub * CHUNK
    in_band = (k_rel >= 0) & (k_rel < A_BAND)
    bias = jnp.where(in_band[None], bias, -jnp.inf)
    return bias.reshape(N_SLABS, HEADS_PER_SLAB * gq, win)


def _dup_lanes(ki):
    return jnp.concatenate([ki, ki], axis=-1)


def kernel(x_prompt, x_sample, cache_a_k, cache_a_v, cache_b_k, cache_b_v, cache_b_kidx,
           w_in, rel_bias, w_out, ln1_g, ln1_b, w_up, w_down, ln2_g, ln2_b):
    B, S, D = x_prompt.shape
    Bs, T, _ = x_sample.shape
    depth = w_in.shape[0]
    p_len = cache_b_k.shape[2]
    a_len = cache_a_k.shape[2]
    assert S % ROW_TILE == 0 and (Bs * T) % ROW_TILE == 0 and T == CHUNK and a_len == A_PAST
    alpha = float((2 * depth) ** 0.25)
    topk_p = min(TOPK_MAX, S // 4)
    topk_s = min(TOPK_MAX, (p_len + T) // 4)
    keep_a = min(A_PAST, S)
    assert keep_a == ROW_TILE
    tiles_per_seq = S // ROW_TILE

    cos_p, sin_p = _rope_tables(jnp.arange(S))
    pos_s = jnp.tile(p_len + jnp.arange(T), ROW_TILE // T)
    cos_s, sin_s = _rope_tables(pos_s)

    tq_p = 2 * CHUNK
    widths_p = tuple(range(ROW_TILE, S + 1, ROW_TILE))
    lk_s = -(-(p_len + T) // LANES) * LANES
    pad_s = lk_s - (p_len + T)

    xp = x_prompt.reshape(B * S, D)
    xs = x_sample.reshape(Bs * T, D)
    outs = {k: [] for k in ("ak_p", "av_p", "bk_p", "bv_p", "bi_p", "ak_s", "av_s", "bk_s", "bv_s", "bi_s")}

    for l in range(depth):
        w = jnp.pad(w_in[l], ((0, 0), (0, IN_PAD - w_in.shape[2]))).astype(BF16)
        wo = w_out[l].astype(BF16)
        wu = w_up[l].astype(BF16)
        wd = w_down[l].astype(BF16)
        g1, b1 = ln1_g[l][None, :], ln1_b[l][None, :]
        g2, b2 = ln2_g[l][None, :], ln2_b[l][None, :]
        bias_p = _band_bias(rel_bias[l], BAND_GROUP_CHUNKS)
        bias_s = _band_bias(rel_bias[l], 1)

        (qa, ka, va, qb, kb, vb, qi, ki2, kaf, vaf, kbf, vbf, small) = _project(xp, w, cos_p, sin_p, tiles_per_seq)
        r3 = lambda a: a.reshape(B, S, a.shape[-1])
        oa = _band_attention(r3(qa), r3(ka), r3(va), bias_p, q_off=0, group_chunks=BAND_GROUP_CHUNKS)
        ob = _sparse_attention(r3(qb), r3(kb), r3(vb), r3(qi), r3(ki2), r3(small), tq=tq_p,
                               widths=widths_p, causal_blocks=True, q_pos_base=0, topk=topk_p)
        xp = _post(xp, oa.reshape(B * S, GROUP_W), ob.reshape(B * S, GROUP_W), wo, g1, b1, wu, wd, g2, b2, alpha)
        outs["ak_p"].append(kaf.reshape(B, keep_a, N_HEADS, HEAD_DIM))
        outs["av_p"].append(vaf.reshape(B, keep_a, N_HEADS, HEAD_DIM))
        outs["bk_p"].append(kbf.reshape(B, S, N_HEADS, HEAD_DIM))
        outs["bv_p"].append(vbf.reshape(B, S, N_HEADS, HEAD_DIM))
        outs["bi_p"].append(small[:, :IDX_DIM].reshape(B, S, IDX_DIM))

        (qa, ka, va, qb, kb, vb, qi, ki2, kaf, vaf, kbf, vbf, small) = _project(xs, w, cos_s, sin_s, 1)
        s3 = lambda a: a.reshape(Bs, T, a.shape[-1])
        cat_a = lambda c, n: jnp.concatenate([c.reshape(Bs, a_len, GROUP_W).astype(BF16), s3(n)], axis=1)
        oa = _band_attention(s3(qa), cat_a(cache_a_k[l], ka), cat_a(cache_a_v[l], va), bias_s,
                             q_off=a_len, group_chunks=1)
        zpad = lambda n: jnp.zeros((Bs, pad_s, n), BF16)
        cat_b = lambda c, n: jnp.concatenate(
            [c.reshape(Bs, p_len, GROUP_W).astype(BF16), s3(n), zpad(GROUP_W)], axis=1)
        ki2_all = jnp.concatenate([_dup_lanes(cache_b_kidx[l]).astype(BF16), s3(ki2), zpad(LANES)], axis=1)
        ob = _sparse_attention(s3(qb), cat_b(cache_b_k[l], kb), cat_b(cache_b_v[l], vb), s3(qi), ki2_all,
                               s3(small), tq=T, widths=(lk_s,), causal_blocks=False, q_pos_base=p_len,
                               topk=topk_s)
        xs = _post(xs, oa.reshape(Bs * T, GROUP_W), ob.reshape(Bs * T, GROUP_W), wo, g1, b1, wu, wd, g2, b2, alpha)
        outs["ak_s"].append(kaf.reshape(Bs, T, N_HEADS, HEAD_DIM))
        outs["av_s"].append(vaf.reshape(Bs, T, N_HEADS, HEAD_DIM))
        outs["bk_s"].append(kbf.reshape(Bs, T, N_HEADS, HEAD_DIM))
        outs["bv_s"].append(vbf.reshape(Bs, T, N_HEADS, HEAD_DIM))
        outs["bi_s"].append(small[:, :IDX_DIM].reshape(Bs, T, IDX_DIM))

    st = lambda k: jnp.stack(outs[k])
    return (xp.reshape(B, S, D), xs.reshape(Bs, T, D),
            st("ak_p"), st("av_p"), st("bk_p"), st("bv_p"), st("bi_p"),
            st("ak_s"), st("av_s"), st("bk_s"), st("bv_s"), st("bi_s"))
```

```python
import functools

import jax
import jax.numpy as jnp
import numpy as np
from jax import lax
from jax.experimental import pallas as pl
from jax.experimental.pallas import tpu as pltpu

F32 = jnp.float32
BF16 = jnp.bfloat16
I32 = jnp.int32
I16 = jnp.int16

CHUNK = 64
HEAD_DIM = 64
N_HEADS = 8
GROUP_W = N_HEADS * HEAD_DIM
A_PAST = 8 * CHUNK
A_BAND = A_PAST + CHUNK
REL_CLIP = 128
IDX_HEADS = 8
IDX_DIM = 64
TOPK_MAX = 256
ROPE_THETA = 10000.0
LN_EPS = 1e-5

LANES = 128
HEADS_PER_SLAB = LANES // HEAD_DIM
N_SLABS = GROUP_W // LANES
ROW_TILE = 512
BAND_GROUP_CHUNKS = 4
SEARCH_BITS = 16
SPARSE_WIDTH_STEP = 256
VMEM_LIMIT = 56 * 1024 * 1024

IN_MAIN = 7 * GROUP_W
IN_SMALL0 = IN_MAIN
IN_PAD = IN_MAIN + LANES

KEY_NEG_INF = np.int32(np.int32(np.float32(-np.inf).view(np.int32)) ^ np.int32(0x7FFFFFFF))


def _cparams(n_axes):
    return pltpu.CompilerParams(dimension_semantics=("arbitrary",) * n_axes,
                                vmem_limit_bytes=VMEM_LIMIT)


def _proj_kernel(*refs, keep_tiles, transposed, n_alias):
    x_ref, w_ref, cos_ref, sin_ref = refs[:4]
    outs = refs[4 + n_alias:]
    qa_o, ka_o, va_o, qb_o, kb_o, vb_o, qi_o, ki2_o, kaf_o, vaf_o, kbf_o, vbf_o, small_o = outs[:13]
    i = pl.program_id(0)
    tm = x_ref.shape[0]
    x = x_ref[...].astype(BF16)
    cos = cos_ref[...]
    sin = sin_ref[...]
    lane = lax.broadcasted_iota(I32, (tm, LANES), 1)
    first_half = (lane & (HEAD_DIM - 1)) < (HEAD_DIM // 2)

    def sec(c0, n):
        return jnp.dot(x, w_ref[:, c0:c0 + n], preferred_element_type=F32)

    def rope_slab(slab):
        swapped = jnp.where(first_half, pltpu.roll(slab, LANES - HEAD_DIM // 2, 1),
                            pltpu.roll(slab, HEAD_DIM // 2, 1))
        return slab * cos + swapped * sin

    def rope(h):
        return jnp.concatenate([rope_slab(h[:, s * LANES:(s + 1) * LANES])
                                for s in range(h.shape[1] // LANES)], axis=1)

    def put_cache(o_ref, val):
        if transposed:
            o_ref[0, 0] = val.T
        else:
            o_ref[...] = val

    last = (i % keep_tiles) == (keep_tiles - 1)

    qa_o[...] = (sec(0 * GROUP_W, GROUP_W) * (HEAD_DIM ** -0.5)).astype(BF16)
    ka = sec(1 * GROUP_W, GROUP_W)
    ka_o[...] = ka.astype(BF16)
    va = sec(2 * GROUP_W, GROUP_W)
    va_o[...] = va.astype(BF16)

    @pl.when(last)
    def _():
        put_cache(kaf_o, ka)
        put_cache(vaf_o, va)

    qb_o[...] = (rope(sec(3 * GROUP_W, GROUP_W)) * (HEAD_DIM ** -0.5)).astype(BF16)
    kb = rope(sec(4 * GROUP_W, GROUP_W))
    put_cache(kbf_o, kb)
    kb_o[...] = kb.astype(BF16)
    vb = sec(5 * GROUP_W, GROUP_W)
    put_cache(vbf_o, vb)
    vb_o[...] = vb.astype(BF16)
    qi_o[...] = rope(sec(6 * GROUP_W, GROUP_W)).astype(BF16)

    sm = sec(IN_SMALL0, LANES)
    ki = rope_slab(sm)
    is_ki = lane < IDX_DIM
    small_o[...] = jnp.where(is_ki, ki, sm * ((IDX_HEADS * IDX_DIM) ** -0.5))
    ki2_o[...] = jnp.where(is_ki, ki, pltpu.roll(ki, IDX_DIM, 1)).astype(BF16)
    if transposed:
        outs[13][0, 0] = ki.T[:IDX_DIM, :]


def _project(x2d, w, cos_t, sin_t, keep_tiles, stacked=None):
    R, D = x2d.shape
    tm = ROW_TILE
    n_tiles = R // tm
    n_pos = cos_t.shape[0] // tm
    row = lambda i: (i, 0)
    bf = lambda n: jax.ShapeDtypeStruct((R, n), BF16)
    ff = lambda r, n: jax.ShapeDtypeStruct((r, n), F32)
    blk = lambda n: pl.BlockSpec((tm, n), row)
    out_shape = [bf(GROUP_W)] * 7 + [bf(LANES)]
    out_specs = [blk(GROUP_W)] * 7 + [blk(LANES)]
    args = [x2d, w, cos_t, sin_t]
    in_specs = [pl.BlockSpec((tm, D), row),
                pl.BlockSpec((D, IN_PAD), lambda i: (0, 0)),
                pl.BlockSpec((tm, LANES), lambda i: (i % n_pos, 0)),
                pl.BlockSpec((tm, LANES), lambda i: (i % n_pos, 0))]
    aliases = {}
    if stacked is None:
        keep_blk = pl.BlockSpec((tm, GROUP_W), lambda i: (i // keep_tiles, 0))
        out_shape += [ff(R // keep_tiles, GROUP_W), ff(R // keep_tiles, GROUP_W),
                      ff(R, GROUP_W), ff(R, GROUP_W), ff(R, LANES)]
        out_specs += [keep_blk, keep_blk, blk(GROUP_W), blk(GROUP_W), blk(LANES)]
    else:
        layer, depth, batch, carried = stacked
        seq = keep_tiles * tm
        tsd = lambda n, t: jax.ShapeDtypeStruct((depth, batch, n, t), F32)
        kept = lambda n: pl.BlockSpec((1, 1, n, tm), lambda i: (layer, i // keep_tiles, 0, 0))
        full = lambda n: pl.BlockSpec((1, 1, n, tm), lambda i: (layer, i // keep_tiles, 0, i % keep_tiles))
        out_shape += [tsd(GROUP_W, tm), tsd(GROUP_W, tm), tsd(GROUP_W, seq), tsd(GROUP_W, seq),
                      ff(R, LANES), tsd(IDX_DIM, seq)]
        out_specs += [kept(GROUP_W), kept(GROUP_W), full(GROUP_W), full(GROUP_W), blk(LANES), full(IDX_DIM)]
        if carried is not None:
            carried_out = (8, 9, 10, 11, 13)
            aliases = {len(args) + k: o for k, o in enumerate(carried_out)}
            args += list(carried)
            in_specs += [pl.BlockSpec(memory_space=pl.ANY)] * len(carried)
    return pl.pallas_call(
        functools.partial(_proj_kernel, keep_tiles=keep_tiles, transposed=stacked is not None,
                          n_alias=len(aliases)),
        grid=(n_tiles,),
        in_specs=in_specs,
        out_specs=out_specs,
        out_shape=out_shape,
        input_output_aliases=aliases,
        compiler_params=_cparams(1),
        name="proj_rope",
    )(*args)


def _head_masks(rows):
    lane = lax.broadcasted_iota(I32, (rows, LANES), 1)
    return [jnp.where((lane >= h * HEAD_DIM) & (lane < (h + 1) * HEAD_DIM), 1.0, 0.0).astype(BF16)
            for h in range(HEADS_PER_SLAB)]


def _band_kernel(q_ref, k_ref, v_ref, bias_ref, o_ref, *, n_groups, group_chunks, q_off):
    gq = group_chunks * CHUNK
    win = A_PAST + gq
    lane = lax.broadcasted_iota(I32, (gq, LANES), 1)
    head_masks = _head_masks(gq)
    for gi in range(n_groups):
        qs = gi * gq
        start = q_off + qs - A_PAST
        lo = max(start, 0)
        hi = q_off + qs + gq
        q2 = q_ref[0, qs:qs + gq, :]
        kw = k_ref[0, lo:hi, :]
        vw = v_ref[0, lo:hi, :]
        qst = jnp.concatenate([q2 * head_masks[h] for h in range(HEADS_PER_SLAB)], axis=0)
        s = lax.dot_general(qst, kw, (((1,), (1,)), ((), ())), preferred_element_type=F32)
        s = s + bias_ref[0, :, lo - start:win]
        m = jnp.max(s, axis=1, keepdims=True)
        p = jnp.exp(s - m)
        l = jnp.sum(p, axis=1, keepdims=True)
        o = jnp.dot(p.astype(BF16), vw, preferred_element_type=F32) / l
        o2 = jnp.where(lane < HEAD_DIM, o[:gq], o[gq:])
        o_ref[0, qs:qs + gq, :] = o2.astype(BF16)


def _band_attention(q, k, v, bias, q_off, group_chunks):
    B, Sq, _ = q.shape
    Sk = k.shape[1]
    gq = group_chunks * CHUNK
    return pl.pallas_call(
        functools.partial(_band_kernel, n_groups=Sq // gq, group_chunks=group_chunks, q_off=q_off),
        grid=(B, N_SLABS),
        in_specs=[pl.BlockSpec((1, Sq, LANES), lambda b, g: (b, 0, g)),
                  pl.BlockSpec((1, Sk, LANES), lambda b, g: (b, 0, g)),
                  pl.BlockSpec((1, Sk, LANES), lambda b, g: (b, 0, g)),
                  pl.BlockSpec((1, HEADS_PER_SLAB * gq, A_PAST + gq), lambda b, g: (g, 0, 0))],
        out_specs=pl.BlockSpec((1, Sq, LANES), lambda b, g: (b, 0, g)),
        out_shape=jax.ShapeDtypeStruct((B, Sq, GROUP_W), BF16),
        compiler_params=_cparams(2),
        name="band_attention",
    )(q, k, v, bias)


def _count16(vals, cand, strict):
    rows, width = vals.shape
    c16 = jnp.broadcast_to(cand, (rows, LANES)).astype(I16)
    hits = []
    for t in range(width // LANES):
        v = vals[:, t * LANES:(t + 1) * LANES]
        hits.append(jnp.where((v > c16) if strict else (v >= c16), jnp.ones((), BF16), jnp.zeros((), BF16)))
    assert len(hits) < 2 ** 8
    while len(hits) > 1:
        hits = [a + b for a, b in zip(hits[::2], hits[1::2])] + ([hits[-1]] if len(hits) % 2 else [])
    return jnp.sum(hits[0].astype(F32), axis=1, keepdims=True)


def _kth_largest16(vals, need):
    rows = vals.shape[0]
    bias = np.int32(2 ** 15)
    u = jnp.zeros((rows, 1), I32)
    for pair in range(SEARCH_BITS // 2 - 1, -1, -1):
        b1, b0 = np.int32(2 << (2 * pair)), np.int32(1 << (2 * pair))
        ok = {c: _count16(vals, u + np.int32(c) - bias, False) >= need for c in (b1, b1 + b0, b0)}
        u = jnp.where(ok[b1], jnp.where(ok[b1 + b0], u + (b1 + b0), u + b1), jnp.where(ok[b0], u + b0, u))
    return u - bias


def _kth_largest_key(key, kf):
    rows, width = key.shape
    hi = (key >> 16).astype(I16)
    lo = ((key & np.int32(0xFFFF)) - np.int32(2 ** 15)).astype(I16)
    p_hi = _kth_largest16(hi, kf)
    need_lo = kf - _count16(hi, p_hi, True)
    p16 = jnp.broadcast_to(p_hi, (rows, LANES)).astype(I16)
    lo_in_bucket = jnp.concatenate(
        [jnp.where(hi[:, t * LANES:(t + 1) * LANES] == p16, lo[:, t * LANES:(t + 1) * LANES], np.int16(-2 ** 15))
         for t in range(width // LANES)], axis=1)
    p_lo = _kth_largest16(lo_in_bucket, need_lo)
    return (p_hi << 16) + (p_lo + np.int32(2 ** 15))


def _admissible(tq, W, q_pos0):
    row = lax.broadcasted_iota(I32, (tq, W), 0)
    col = lax.broadcasted_iota(I32, (tq, W), 1)
    chunk_shift = CHUNK.bit_length() - 1
    return (col >> chunk_shift) <= ((row + q_pos0) >> chunk_shift)


def _topk_negmask(qi_ref, ki2_ref, small_ref, *, width, q_pos0, topk):
    tq = qi_ref.shape[1]
    W = width
    head_masks = _head_masks(tq)

    small = small_ref[0]
    qi = qi_ref[0]
    ki2 = ki2_ref[0, :W, :]
    stacked = []
    for g in range(N_SLABS):
        q2 = qi[:, g * LANES:(g + 1) * LANES]
        for h in range(HEADS_PER_SLAB):
            stacked.append(q2 * head_masks[h])
    qstack = jnp.concatenate(stacked, axis=0)
    dots = lax.dot_general(qstack, ki2, (((1,), (1,)), ((), ())), preferred_element_type=F32)
    score = jnp.zeros((tq, W), F32)
    for h in range(IDX_HEADS):
        wcol = small[:, IDX_DIM + h:IDX_DIM + h + 1]
        score = score + wcol * jnp.maximum(dots[h * tq:(h + 1) * tq, :], 0.0)

    adm = _admissible(tq, W, q_pos0)
    score = jnp.where(adm, score + 0.0, -jnp.inf)

    bits = pltpu.bitcast(score, I32)
    key = bits ^ ((bits >> 31) & np.int32(0x7FFFFFFF))
    kf = np.float32(topk)
    thr = _kth_largest_key(key, kf)

    thr = jnp.maximum(thr, KEY_NEG_INF + np.int32(1))
    ge = key >= thr
    n_ge = jnp.sum(jnp.where(ge, 1.0, 0.0), axis=1, keepdims=True)
    sel_simple = jnp.where(ge, 0.0, -jnp.inf)

    def tie_path():
        gt = key > thr
        eq = key == thr
        need = kf - jnp.sum(jnp.where(gt, 1.0, 0.0), axis=1, keepdims=True)
        eqf = jnp.where(eq, 1.0, 0.0).astype(BF16)
        ui = lax.broadcasted_iota(I32, (LANES, LANES), 0)
        uj = lax.broadcasted_iota(I32, (LANES, LANES), 1)
        upper = jnp.where(ui < uj, 1.0, 0.0).astype(BF16)
        ones = jnp.ones((LANES, LANES), BF16)
        offset = jnp.zeros((tq, 1), F32)
        pieces = []
        for t in range(W // LANES):
            e = eqf[:, t * LANES:(t + 1) * LANES]
            rank = offset + jnp.dot(e, upper, preferred_element_type=F32)
            pieces.append(rank < need)
            offset = offset + jnp.dot(e, ones, preferred_element_type=F32)[:, :1]
        tie_ok = jnp.concatenate(pieces, axis=1)
        return jnp.where(gt | (eq & tie_ok), 0.0, -jnp.inf)

    any_excess = jnp.max(n_ge) > kf
    return lax.cond(any_excess, tie_path, lambda: sel_simple)


def _masked_attention(qb_ref, kb_ref, vb_ref, o_ref, negmask, *, width):
    tq = qb_ref.shape[1]
    W = width
    lane = lax.broadcasted_iota(I32, (tq, LANES), 1)
    head_masks = _head_masks(tq)
    qb = qb_ref[0]
    mask2 = jnp.concatenate([negmask] * HEADS_PER_SLAB, axis=0)
    for g in range(N_SLABS):
        q2 = qb[:, g * LANES:(g + 1) * LANES]
        k2 = kb_ref[0, :W, g * LANES:(g + 1) * LANES]
        v2 = vb_ref[0, :W, g * LANES:(g + 1) * LANES]
        qst = jnp.concatenate([q2 * head_masks[h] for h in range(HEADS_PER_SLAB)], axis=0)
        s = lax.dot_general(qst, k2, (((1,), (1,)), ((), ())), preferred_element_type=F32) + mask2
        m = jnp.max(s, axis=1, keepdims=True)
        p = jnp.exp(s - m)
        l = jnp.sum(p, axis=1, keepdims=True)
        o = jnp.dot(p.astype(BF16), v2, preferred_element_type=F32) / l
        o2 = jnp.where(lane < HEAD_DIM, o[:tq], o[tq:])
        o_ref[0, :, g * LANES:(g + 1) * LANES] = o2.astype(BF16)


def _sparse_kernel(qb_ref, kb_ref, vb_ref, qi_ref, ki2_ref, small_ref, o_ref, *,
                   widths, causal_blocks, q_pos_base, topk):
    tq = qb_ref.shape[1]
    j = pl.program_id(1)

    def body(W, q_pos0):
        negmask = _topk_negmask(qi_ref, ki2_ref, small_ref, width=W, q_pos0=q_pos0, topk=topk)
        _masked_attention(qb_ref, kb_ref, vb_ref, o_ref, negmask, width=W)

    if not causal_blocks:
        body(widths[0], q_pos_base)
        return
    q_pos0 = j * tq
    n_keys = (j + 1) * tq
    dense_w = -(-topk // LANES) * LANES

    @pl.when(n_keys <= topk)
    def _():
        negmask = jnp.where(_admissible(tq, dense_w, q_pos0), 0.0, -jnp.inf)
        _masked_attention(qb_ref, kb_ref, vb_ref, o_ref, negmask, width=dense_w)

    lo = topk
    for W in widths:
        if W <= lo:
            continue

        @pl.when((n_keys > lo) & (n_keys <= W))
        def _(W=W):
            body(W, q_pos0)
        lo = max(W, topk)


def _sparse_attention(qb, kb, vb, qi, ki2, small, *, tq, widths, causal_blocks, q_pos_base, topk):
    B, Sq, _ = qb.shape
    Lk = kb.shape[1]
    nq = Sq // tq
    qspec = lambda n: pl.BlockSpec((1, tq, n), lambda b, j: (b, j, 0))
    kspec = lambda n: pl.BlockSpec((1, Lk, n), lambda b, j: (b, 0, 0))
    return pl.pallas_call(
        functools.partial(_sparse_kernel, widths=widths, causal_blocks=causal_blocks,
                          q_pos_base=q_pos_base, topk=topk),
        grid=(B, nq),
        in_specs=[qspec(GROUP_W), kspec(GROUP_W), kspec(GROUP_W), qspec(GROUP_W), kspec(LANES), qspec(LANES)],
        out_specs=qspec(GROUP_W),
        out_shape=jax.ShapeDtypeStruct((B, Sq, GROUP_W), BF16),
        compiler_params=_cparams(2),
        name="sparse_attention",
    )(qb, kb, vb, qi, ki2, small)


def _layer_norm(x, g, b):
    mu = jnp.mean(x, axis=1, keepdims=True)
    xc = x - mu
    var = jnp.mean(xc * xc, axis=1, keepdims=True)
    return xc * lax.rsqrt(var + LN_EPS) * g + b


def _post_kernel(x_ref, oa_ref, ob_ref, wo_ref, g1_ref, b1_ref, wu_ref, wd_ref, g2_ref, b2_ref, y_ref,
                 *, alpha, ff_chunk):
    x = x_ref[...]
    mix = jnp.dot(oa_ref[...], wo_ref[:GROUP_W, :], preferred_element_type=F32)
    mix = mix + jnp.dot(ob_ref[...], wo_ref[GROUP_W:, :], preferred_element_type=F32)
    x1 = _layer_norm(alpha * x + mix, g1_ref[...], b1_ref[...])
    x1b = x1.astype(BF16)
    d_ff = wu_ref.shape[1]
    ff = jnp.zeros_like(x1)
    for c in range(d_ff // ff_chunk):
        hid = jnp.dot(x1b, wu_ref[:, c * ff_chunk:(c + 1) * ff_chunk], preferred_element_type=F32)
        hid = jnp.maximum(hid, 0.0)
        hid = (hid * hid).astype(BF16)
        ff = ff + jnp.dot(hid, wd_ref[c * ff_chunk:(c + 1) * ff_chunk, :], preferred_element_type=F32)
    y_ref[...] = _layer_norm(alpha * x1 + ff, g2_ref[...], b2_ref[...])


def _post(x2d, oa, ob, wo, g1, b1, wu, wd, g2, b2, alpha):
    R, D = x2d.shape
    tm = ROW_TILE
    d_ff = wu.shape[1]
    row = lambda i: (i, 0)
    const = lambda i: (0, 0)
    return pl.pallas_call(
        functools.partial(_post_kernel, alpha=alpha, ff_chunk=1024),
        grid=(R // tm,),
        in_specs=[pl.BlockSpec((tm, D), row),
                  pl.BlockSpec((tm, GROUP_W), row),
                  pl.BlockSpec((tm, GROUP_W), row),
                  pl.BlockSpec((2 * GROUP_W, D), const),
                  pl.BlockSpec((1, D), const), pl.BlockSpec((1, D), const),
                  pl.BlockSpec((D, d_ff), const),
                  pl.BlockSpec((d_ff, D), const),
                  pl.BlockSpec((1, D), const), pl.BlockSpec((1, D), const)],
        out_specs=pl.BlockSpec((tm, D), row),
        out_shape=jax.ShapeDtypeStruct((R, D), F32),
        compiler_params=_cparams(1),
        name="post_sublayers",
    )(x2d, oa, ob, wo, g1, b1, wu, wd, g2, b2)


def _rope_tables(pos):
    half = HEAD_DIM // 2
    inv = ROPE_THETA ** (-jnp.arange(half, dtype=F32) / half)
    ang = pos.astype(F32)[:, None] * inv[None, :]
    cos, sin = jnp.cos(ang), jnp.sin(ang)
    cos_h = jnp.concatenate([cos, cos], axis=1)
    sin_h = jnp.concatenate([-sin, sin], axis=1)
    return jnp.tile(cos_h, (1, HEADS_PER_SLAB)), jnp.tile(sin_h, (1, HEADS_PER_SLAB))


def _band_bias(table, group_chunks):
    gq = group_chunks * CHUNK
    win = A_PAST + gq
    d_min, d_max = A_PAST - (win - 1), gq - 1 + A_PAST
    n = d_max - d_min + 1
    left = jnp.repeat(table[:, :1], max(0, -REL_CLIP - d_min), axis=1)
    mid = table[:, max(d_min, -REL_CLIP) + REL_CLIP:min(d_max, REL_CLIP) + REL_CLIP + 1]
    right = jnp.repeat(table[:, -1:], max(0, d_max - REL_CLIP), axis=1)
    ext = jnp.concatenate([left, mid, right], axis=1).astype(F32)
    rp = jnp.concatenate([ext[:, ::-1], jnp.zeros((N_HEADS, 1), F32)], axis=1)
    shifted = jnp.tile(rp, (1, gq))[:, :gq * n].reshape(N_HEADS, gq, n)
    bias = shifted[:, :, gq - 1:gq - 1 + win]
    q_sub = jnp.arange(gq)[:, None] // CHUNK
    k_rel = jnp.arange(win)[None, :] - q_sub * CHUNK
    in_band = (k_rel >= 0) & (k_rel < A_BAND)
    bias = jnp.where(in_band[None], bias, -jnp.inf)
    return bias.reshape(N_SLABS, HEADS_PER_SLAB * gq, win)


def _dup_lanes(ki):
    return jnp.concatenate([ki, ki], axis=-1)


def kernel(x_prompt, x_sample, cache_a_k, cache_a_v, cache_b_k, cache_b_v, cache_b_kidx,
           w_in, rel_bias, w_out, ln1_g, ln1_b, w_up, w_down, ln2_g, ln2_b):
    B, S, D = x_prompt.shape
    Bs, T, _ = x_sample.shape
    depth = w_in.shape[0]
    p_len = cache_b_k.shape[2]
    a_len = cache_a_k.shape[2]
    assert S % ROW_TILE == 0 and (Bs * T) % ROW_TILE == 0 and T == CHUNK and a_len == A_PAST
    alpha = float((2 * depth) ** 0.25)
    topk_p = min(TOPK_MAX, S // 4)
    topk_s = min(TOPK_MAX, (p_len + T) // 4)
    keep_a = min(A_PAST, S)
    assert keep_a == ROW_TILE
    tiles_per_seq = S // ROW_TILE

    cos_p, sin_p = _rope_tables(jnp.arange(S))
    pos_s = jnp.tile(p_len + jnp.arange(T), ROW_TILE // T)
    cos_s, sin_s = _rope_tables(pos_s)

    tq_p = 2 * CHUNK
    widths_p = tuple(range(SPARSE_WIDTH_STEP, S + 1, SPARSE_WIDTH_STEP))
    lk_s = -(-(p_len + T) // LANES) * LANES
    pad_s = lk_s - (p_len + T)

    xp = x_prompt.reshape(B * S, D)
    xs = x_sample.reshape(Bs * T, D)
    outs = {k: [] for k in ("ak_s", "av_s", "bk_s", "bv_s", "bi_s")}
    carried = None

    for l in range(depth):
        w = jnp.pad(w_in[l], ((0, 0), (0, IN_PAD - w_in.shape[2]))).astype(BF16)
        wo = w_out[l].astype(BF16)
        wu = w_up[l].astype(BF16)
        wd = w_down[l].astype(BF16)
        g1, b1 = ln1_g[l][None, :], ln1_b[l][None, :]
        g2, b2 = ln2_g[l][None, :], ln2_b[l][None, :]
        bias_p = _band_bias(rel_bias[l], BAND_GROUP_CHUNKS)
        bias_s = _band_bias(rel_bias[l], 1)

        (qa, ka, va, qb, kb, vb, qi, ki2, *cache_p) = _project(xp, w, cos_p, sin_p, tiles_per_seq,
                                                                stacked=(l, depth, B, carried))
        small = cache_p[4]
        carried = cache_p[:4] + cache_p[5:]
        r3 = lambda a: a.reshape(B, S, a.shape[-1])
        oa = _band_attention(r3(qa), r3(ka), r3(va), bias_p, q_off=0, group_chunks=BAND_GROUP_CHUNKS)
        ob = _sparse_attention(r3(qb), r3(kb), r3(vb), r3(qi), r3(ki2), r3(small), tq=tq_p,
                               widths=widths_p, causal_blocks=True, q_pos_base=0, topk=topk_p)
        xp = _post(xp, oa.reshape(B * S, GROUP_W), ob.reshape(B * S, GROUP_W), wo, g1, b1, wu, wd, g2, b2, alpha)

        (qa, ka, va, qb, kb, vb, qi, ki2, kaf, vaf, kbf, vbf, small) = _project(xs, w, cos_s, sin_s, 1)
        s3 = lambda a: a.reshape(Bs, T, a.shape[-1])
        cat_a = lambda c, n: jnp.concatenate([c.reshape(Bs, a_len, GROUP_W).astype(BF16), s3(n)], axis=1)
        oa = _band_attention(s3(qa), cat_a(cache_a_k[l], ka), cat_a(cache_a_v[l], va), bias_s,
                             q_off=a_len, group_chunks=1)
        zpad = lambda n: jnp.zeros((Bs, pad_s, n), BF16)
        cat_b = lambda c, n: jnp.concatenate(
            [c.reshape(Bs, p_len, GROUP_W).astype(BF16), s3(n), zpad(GROUP_W)], axis=1)
        ki2_all = jnp.concatenate([_dup_lanes(cache_b_kidx[l]).astype(BF16), s3(ki2), zpad(LANES)], axis=1)
        ob = _sparse_attention(s3(qb), cat_b(cache_b_k[l], kb), cat_b(cache_b_v[l], vb), s3(qi), ki2_all,
                               s3(small), tq=T, widths=(lk_s,), causal_blocks=False, q_pos_base=p_len,
                               topk=topk_s)
        xs = _post(xs, oa.reshape(Bs * T, GROUP_W), ob.reshape(Bs * T, GROUP_W), wo, g1, b1, wu, wd, g2, b2, alpha)
        outs["ak_s"].append(kaf.reshape(Bs, T, N_HEADS, HEAD_DIM))
        outs["av_s"].append(vaf.reshape(Bs, T, N_HEADS, HEAD_DIM))
        outs["bk_s"].append(kbf.reshape(Bs, T, N_HEADS, HEAD_DIM))
        outs["bv_s"].append(vbf.reshape(Bs, T, N_HEADS, HEAD_DIM))
        outs["bi_s"].append(small[:, :IDX_DIM].reshape(Bs, T, IDX_DIM))

    ak_t, av_t, bk_t, bv_t, bi_t = carried
    heads_last = lambda a: a.reshape(depth, B, N_HEADS, HEAD_DIM, a.shape[-1]).transpose(0, 1, 4, 2, 3)
    st = lambda k: jnp.stack(outs[k])
    return (xp.reshape(B, S, D), xs.reshape(Bs, T, D),
            heads_last(ak_t), heads_last(av_t), heads_last(bk_t), heads_last(bv_t), bi_t.transpose(0, 1, 3, 2),
            st("ak_s"), st("av_s"), st("bk_s"), st("bv_s"), st("bi_s"))
```

```python
import functools

import jax
import jax.numpy as jnp
import numpy as np
from jax import lax
from jax.experimental import pallas as pl
from jax.experimental.pallas import tpu as pltpu

F32 = jnp.float32
BF16 = jnp.bfloat16
I32 = jnp.int32
I16 = jnp.int16

CHUNK = 64
HEAD_DIM = 64
N_HEADS = 8
GROUP_W = N_HEADS * HEAD_DIM
A_PAST = 8 * CHUNK
A_BAND = A_PAST + CHUNK
REL_CLIP = 128
IDX_HEADS = 8
IDX_DIM = 64
TOPK_MAX = 256
ROPE_THETA = 10000.0
LN_EPS = 1e-5

LANES = 128
HEADS_PER_SLAB = LANES // HEAD_DIM
N_SLABS = GROUP_W // LANES
ROW_TILE = 512
BAND_GROUP_CHUNKS = 4
SEARCH_BITS = 16
SPARSE_WIDTH_STEP = 512
VMEM_LIMIT = 56 * 1024 * 1024

IN_MAIN = 7 * GROUP_W
IN_SMALL0 = IN_MAIN
IN_PAD = IN_MAIN + LANES

KEY_NEG_INF = np.int32(np.int32(np.float32(-np.inf).view(np.int32)) ^ np.int32(0x7FFFFFFF))


def _cparams(n_axes):
    return pltpu.CompilerParams(dimension_semantics=("arbitrary",) * n_axes,
                                vmem_limit_bytes=VMEM_LIMIT)


def _proj_kernel(*refs, keep_tiles, transposed, n_alias):
    x_ref, w_ref, cos_ref, sin_ref = refs[:4]
    outs = refs[4 + n_alias:]
    qa_o, ka_o, va_o, qb_o, kb_o, vb_o, qi_o, ki2_o, kaf_o, vaf_o, kbf_o, vbf_o, small_o = outs[:13]
    i = pl.program_id(0)
    tm = x_ref.shape[0]
    x = x_ref[...].astype(BF16)
    cos = cos_ref[...]
    sin = sin_ref[...]
    lane = lax.broadcasted_iota(I32, (tm, LANES), 1)
    first_half = (lane & (HEAD_DIM - 1)) < (HEAD_DIM // 2)

    def sec(c0, n):
        return jnp.dot(x, w_ref[:, c0:c0 + n], preferred_element_type=F32)

    def rope_slab(slab):
        swapped = jnp.where(first_half, pltpu.roll(slab, LANES - HEAD_DIM // 2, 1),
                            pltpu.roll(slab, HEAD_DIM // 2, 1))
        return slab * cos + swapped * sin

    def rope(h):
        return jnp.concatenate([rope_slab(h[:, s * LANES:(s + 1) * LANES])
                                for s in range(h.shape[1] // LANES)], axis=1)

    def put_cache(o_ref, val):
        if transposed:
            o_ref[0, 0] = val.T
        else:
            o_ref[...] = val

    last = (i % keep_tiles) == (keep_tiles - 1)

    qa_o[...] = (sec(0 * GROUP_W, GROUP_W) * (HEAD_DIM ** -0.5)).astype(BF16)
    ka = sec(1 * GROUP_W, GROUP_W)
    ka_o[...] = ka.astype(BF16)
    va = sec(2 * GROUP_W, GROUP_W)
    va_o[...] = va.astype(BF16)

    @pl.when(last)
    def _():
        put_cache(kaf_o, ka)
        put_cache(vaf_o, va)

    qb_o[...] = (rope(sec(3 * GROUP_W, GROUP_W)) * (HEAD_DIM ** -0.5)).astype(BF16)
    kb = rope(sec(4 * GROUP_W, GROUP_W))
    put_cache(kbf_o, kb)
    kb_o[...] = kb.astype(BF16)
    vb = sec(5 * GROUP_W, GROUP_W)
    put_cache(vbf_o, vb)
    vb_o[...] = vb.astype(BF16)
    qi_o[...] = rope(sec(6 * GROUP_W, GROUP_W)).astype(BF16)

    sm = sec(IN_SMALL0, LANES)
    ki = rope_slab(sm)
    is_ki = lane < IDX_DIM
    small_o[...] = jnp.where(is_ki, ki, sm * ((IDX_HEADS * IDX_DIM) ** -0.5))
    ki2_o[...] = jnp.where(is_ki, ki, pltpu.roll(ki, IDX_DIM, 1)).astype(BF16)
    if transposed:
        outs[13][0, 0] = ki.T[:IDX_DIM, :]


def _project(x2d, w, cos_t, sin_t, keep_tiles, stacked=None):
    R, D = x2d.shape
    tm = ROW_TILE
    n_tiles = R // tm
    n_pos = cos_t.shape[0] // tm
    row = lambda i: (i, 0)
    bf = lambda n: jax.ShapeDtypeStruct((R, n), BF16)
    ff = lambda r, n: jax.ShapeDtypeStruct((r, n), F32)
    blk = lambda n: pl.BlockSpec((tm, n), row)
    out_shape = [bf(GROUP_W)] * 7 + [bf(LANES)]
    out_specs = [blk(GROUP_W)] * 7 + [blk(LANES)]
    args = [x2d, w, cos_t, sin_t]
    in_specs = [pl.BlockSpec((tm, D), row),
                pl.BlockSpec((D, IN_PAD), lambda i: (0, 0)),
                pl.BlockSpec((tm, LANES), lambda i: (i % n_pos, 0)),
                pl.BlockSpec((tm, LANES), lambda i: (i % n_pos, 0))]
    aliases = {}
    if stacked is None:
        keep_blk = pl.BlockSpec((tm, GROUP_W), lambda i: (i // keep_tiles, 0))
        out_shape += [ff(R // keep_tiles, GROUP_W), ff(R // keep_tiles, GROUP_W),
                      ff(R, GROUP_W), ff(R, GROUP_W), ff(R, LANES)]
        out_specs += [keep_blk, keep_blk, blk(GROUP_W), blk(GROUP_W), blk(LANES)]
    else:
        layer, depth, batch, carried = stacked
        seq = keep_tiles * tm
        tsd = lambda n, t: jax.ShapeDtypeStruct((depth, batch, n, t), F32)
        kept = lambda n: pl.BlockSpec((1, 1, n, tm), lambda i: (layer, i // keep_tiles, 0, 0))
        full = lambda n: pl.BlockSpec((1, 1, n, tm), lambda i: (layer, i // keep_tiles, 0, i % keep_tiles))
        out_shape += [tsd(GROUP_W, tm), tsd(GROUP_W, tm), tsd(GROUP_W, seq), tsd(GROUP_W, seq),
                      ff(R, LANES), tsd(IDX_DIM, seq)]
        out_specs += [kept(GROUP_W), kept(GROUP_W), full(GROUP_W), full(GROUP_W), blk(LANES), full(IDX_DIM)]
        if carried is not None:
            carried_out = (8, 9, 10, 11, 13)
            aliases = {len(args) + k: o for k, o in enumerate(carried_out)}
            args += list(carried)
            in_specs += [pl.BlockSpec(memory_space=pl.ANY)] * len(carried)
    return pl.pallas_call(
        functools.partial(_proj_kernel, keep_tiles=keep_tiles, transposed=stacked is not None,
                          n_alias=len(aliases)),
        grid=(n_tiles,),
        in_specs=in_specs,
        out_specs=out_specs,
        out_shape=out_shape,
        input_output_aliases=aliases,
        compiler_params=_cparams(1),
        name="proj_rope",
    )(*args)


def _head_masks(rows):
    lane = lax.broadcasted_iota(I32, (rows, LANES), 1)
    return [jnp.where((lane >= h * HEAD_DIM) & (lane < (h + 1) * HEAD_DIM), 1.0, 0.0).astype(BF16)
            for h in range(HEADS_PER_SLAB)]


def _band_kernel(q_ref, k_ref, v_ref, bias_ref, o_ref, *, n_groups, group_chunks, q_off):
    gq = group_chunks * CHUNK
    win = A_PAST + gq
    lane = lax.broadcasted_iota(I32, (gq, LANES), 1)
    head_masks = _head_masks(gq)
    for gi in range(n_groups):
        qs = gi * gq
        start = q_off + qs - A_PAST
        lo = max(start, 0)
        hi = q_off + qs + gq
        q2 = q_ref[0, qs:qs + gq, :]
        kw = k_ref[0, lo:hi, :]
        vw = v_ref[0, lo:hi, :]
        qst = jnp.concatenate([q2 * head_masks[h] for h in range(HEADS_PER_SLAB)], axis=0)
        s = lax.dot_general(qst, kw, (((1,), (1,)), ((), ())), preferred_element_type=F32)
        s = s + bias_ref[0, :, lo - start:win]
        m = jnp.max(s, axis=1, keepdims=True)
        p = jnp.exp(s - m)
        l = jnp.sum(p, axis=1, keepdims=True)
        o = jnp.dot(p.astype(BF16), vw, preferred_element_type=F32) / l
        o2 = jnp.where(lane < HEAD_DIM, o[:gq], o[gq:])
        o_ref[0, qs:qs + gq, :] = o2.astype(BF16)


def _band_attention(q, k, v, bias, q_off, group_chunks):
    B, Sq, _ = q.shape
    Sk = k.shape[1]
    gq = group_chunks * CHUNK
    return pl.pallas_call(
        functools.partial(_band_kernel, n_groups=Sq // gq, group_chunks=group_chunks, q_off=q_off),
        grid=(B, N_SLABS),
        in_specs=[pl.BlockSpec((1, Sq, LANES), lambda b, g: (b, 0, g)),
                  pl.BlockSpec((1, Sk, LANES), lambda b, g: (b, 0, g)),
                  pl.BlockSpec((1, Sk, LANES), lambda b, g: (b, 0, g)),
                  pl.BlockSpec((1, HEADS_PER_SLAB * gq, A_PAST + gq), lambda b, g: (g, 0, 0))],
        out_specs=pl.BlockSpec((1, Sq, LANES), lambda b, g: (b, 0, g)),
        out_shape=jax.ShapeDtypeStruct((B, Sq, GROUP_W), BF16),
        compiler_params=_cparams(2),
        name="band_attention",
    )(q, k, v, bias)


def _count16(vals, cand, strict):
    rows, width = vals.shape
    c16 = jnp.broadcast_to(cand, (rows, LANES)).astype(I16)
    hits = []
    for t in range(width // LANES):
        v = vals[:, t * LANES:(t + 1) * LANES]
        hits.append(jnp.where((v > c16) if strict else (v >= c16), jnp.ones((), BF16), jnp.zeros((), BF16)))
    assert len(hits) < 2 ** 8
    while len(hits) > 1:
        hits = [a + b for a, b in zip(hits[::2], hits[1::2])] + ([hits[-1]] if len(hits) % 2 else [])
    return jnp.sum(hits[0].astype(F32), axis=1, keepdims=True)


def _kth_largest16(vals, need):
    rows = vals.shape[0]
    bias = np.int32(2 ** 15)
    n_rounds = SEARCH_BITS // 2

    def search_round(it, u):
        b0 = np.int32(1) << (np.int32(2) * (np.int32(n_rounds - 1) - it))
        b1 = b0 + b0
        ok1, ok3, ok0 = (_count16(vals, u + c - bias, False) >= need for c in (b1, b1 + b0, b0))
        return jnp.where(ok1, jnp.where(ok3, u + (b1 + b0), u + b1), jnp.where(ok0, u + b0, u))

    return lax.fori_loop(0, n_rounds, search_round, jnp.zeros((rows, 1), I32)) - bias


def _kth_largest_key(key, kf):
    rows, width = key.shape
    hi = (key >> 16).astype(I16)
    lo = ((key & np.int32(0xFFFF)) - np.int32(2 ** 15)).astype(I16)
    p_hi = _kth_largest16(hi, kf)
    need_lo = kf - _count16(hi, p_hi, True)
    p16 = jnp.broadcast_to(p_hi, (rows, LANES)).astype(I16)
    lo_in_bucket = jnp.concatenate(
        [jnp.where(hi[:, t * LANES:(t + 1) * LANES] == p16, lo[:, t * LANES:(t + 1) * LANES], np.int16(-2 ** 15))
         for t in range(width // LANES)], axis=1)
    p_lo = _kth_largest16(lo_in_bucket, need_lo)
    return (p_hi << 16) + (p_lo + np.int32(2 ** 15))


def _admissible(tq, W, q_pos0):
    row = lax.broadcasted_iota(I32, (tq, W), 0)
    col = lax.broadcasted_iota(I32, (tq, W), 1)
    chunk_shift = CHUNK.bit_length() - 1
    return (col >> chunk_shift) <= ((row + q_pos0) >> chunk_shift)


def _topk_negmask(qi_ref, ki2_ref, small_ref, *, width, q_pos0, topk):
    tq = qi_ref.shape[1]
    W = width
    head_masks = _head_masks(tq)

    small = small_ref[0]
    qi = qi_ref[0]
    ki2 = ki2_ref[0, :W, :]
    stacked = []
    for g in range(N_SLABS):
        q2 = qi[:, g * LANES:(g + 1) * LANES]
        for h in range(HEADS_PER_SLAB):
            stacked.append(q2 * head_masks[h])
    qstack = jnp.concatenate(stacked, axis=0)
    dots = lax.dot_general(qstack, ki2, (((1,), (1,)), ((), ())), preferred_element_type=F32)
    score = jnp.zeros((tq, W), F32)
    for h in range(IDX_HEADS):
        wcol = small[:, IDX_DIM + h:IDX_DIM + h + 1]
        score = score + wcol * jnp.maximum(dots[h * tq:(h + 1) * tq, :], 0.0)

    adm = _admissible(tq, W, q_pos0)
    score = jnp.where(adm, score + 0.0, -jnp.inf)

    bits = pltpu.bitcast(score, I32)
    key = bits ^ ((bits >> 31) & np.int32(0x7FFFFFFF))
    kf = np.float32(topk)
    thr = _kth_largest_key(key, kf)

    thr = jnp.maximum(thr, KEY_NEG_INF + np.int32(1))
    ge = key >= thr
    n_ge = jnp.sum(jnp.where(ge, 1.0, 0.0), axis=1, keepdims=True)
    sel_simple = jnp.where(ge, 0.0, -jnp.inf)

    def tie_path():
        gt = key > thr
        eq = key == thr
        need = kf - jnp.sum(jnp.where(gt, 1.0, 0.0), axis=1, keepdims=True)
        eqf = jnp.where(eq, 1.0, 0.0).astype(BF16)
        ui = lax.broadcasted_iota(I32, (LANES, LANES), 0)
        uj = lax.broadcasted_iota(I32, (LANES, LANES), 1)
        upper = jnp.where(ui < uj, 1.0, 0.0).astype(BF16)
        ones = jnp.ones((LANES, LANES), BF16)
        offset = jnp.zeros((tq, 1), F32)
        pieces = []
        for t in range(W // LANES):
            e = eqf[:, t * LANES:(t + 1) * LANES]
            rank = offset + jnp.dot(e, upper, preferred_element_type=F32)
            pieces.append(rank < need)
            offset = offset + jnp.dot(e, ones, preferred_element_type=F32)[:, :1]
        tie_ok = jnp.concatenate(pieces, axis=1)
        return jnp.where(gt | (eq & tie_ok), 0.0, -jnp.inf)

    any_excess = jnp.max(n_ge) > kf
    return lax.cond(any_excess, tie_path, lambda: sel_simple)


def _masked_attention(qb_ref, kb_ref, vb_ref, o_ref, negmask, *, width):
    tq = qb_ref.shape[1]
    W = width
    lane = lax.broadcasted_iota(I32, (tq, LANES), 1)
    head_masks = _head_masks(tq)
    qb = qb_ref[0]
    mask2 = jnp.concatenate([negmask] * HEADS_PER_SLAB, axis=0)
    for g in range(N_SLABS):
        q2 = qb[:, g * LANES:(g + 1) * LANES]
        k2 = kb_ref[0, :W, g * LANES:(g + 1) * LANES]
        v2 = vb_ref[0, :W, g * LANES:(g + 1) * LANES]
        qst = jnp.concatenate([q2 * head_masks[h] for h in range(HEADS_PER_SLAB)], axis=0)
        s = lax.dot_general(qst, k2, (((1,), (1,)), ((), ())), preferred_element_type=F32) + mask2
        m = jnp.max(s, axis=1, keepdims=True)
        p = jnp.exp(s - m)
        l = jnp.sum(p, axis=1, keepdims=True)
        o = jnp.dot(p.astype(BF16), v2, preferred_element_type=F32) / l
        o2 = jnp.where(lane < HEAD_DIM, o[:tq], o[tq:])
        o_ref[0, :, g * LANES:(g + 1) * LANES] = o2.astype(BF16)


def _sparse_kernel(qb_ref, kb_ref, vb_ref, qi_ref, ki2_ref, small_ref, o_ref, *,
                   widths, causal_blocks, q_pos_base, topk):
    tq = qb_ref.shape[1]
    j = pl.program_id(1)

    def body(W, q_pos0):
        negmask = _topk_negmask(qi_ref, ki2_ref, small_ref, width=W, q_pos0=q_pos0, topk=topk)
        _masked_attention(qb_ref, kb_ref, vb_ref, o_ref, negmask, width=W)

    if not causal_blocks:
        body(widths[0], q_pos_base)
        return
    q_pos0 = j * tq
    n_keys = (j + 1) * tq
    dense_w = -(-topk // LANES) * LANES

    @pl.when(n_keys <= topk)
    def _():
        negmask = jnp.where(_admissible(tq, dense_w, q_pos0), 0.0, -jnp.inf)
        _masked_attention(qb_ref, kb_ref, vb_ref, o_ref, negmask, width=dense_w)

    lo = topk
    for W in widths:
        if W <= lo:
            continue

        @pl.when((n_keys > lo) & (n_keys <= W))
        def _(W=W):
            body(W, q_pos0)
        lo = max(W, topk)


def _sparse_attention(qb, kb, vb, qi, ki2, small, *, tq, widths, causal_blocks, q_pos_base, topk):
    B, Sq, _ = qb.shape
    Lk = kb.shape[1]
    nq = Sq // tq
    qspec = lambda n: pl.BlockSpec((1, tq, n), lambda b, j: (b, j, 0))
    kspec = lambda n: pl.BlockSpec((1, Lk, n), lambda b, j: (b, 0, 0))
    return pl.pallas_call(
        functools.partial(_sparse_kernel, widths=widths, causal_blocks=causal_blocks,
                          q_pos_base=q_pos_base, topk=topk),
        grid=(B, nq),
        in_specs=[qspec(GROUP_W), kspec(GROUP_W), kspec(GROUP_W), qspec(GROUP_W), kspec(LANES), qspec(LANES)],
        out_specs=qspec(GROUP_W),
        out_shape=jax.ShapeDtypeStruct((B, Sq, GROUP_W), BF16),
        compiler_params=_cparams(2),
        name="sparse_attention",
    )(qb, kb, vb, qi, ki2, small)


def _layer_norm(x, g, b):
    mu = jnp.mean(x, axis=1, keepdims=True)
    xc = x - mu
    var = jnp.mean(xc * xc, axis=1, keepdims=True)
    return xc * lax.rsqrt(var + LN_EPS) * g + b


def _post_kernel(x_ref, oa_ref, ob_ref, wo_ref, g1_ref, b1_ref, wu_ref, wd_ref, g2_ref, b2_ref, y_ref,
                 *, alpha, ff_chunk):
    x = x_ref[...]
    mix = jnp.dot(oa_ref[...], wo_ref[:GROUP_W, :], preferred_element_type=F32)
    mix = mix + jnp.dot(ob_ref[...], wo_ref[GROUP_W:, :], preferred_element_type=F32)
    x1 = _layer_norm(alpha * x + mix, g1_ref[...], b1_ref[...])
    x1b = x1.astype(BF16)
    d_ff = wu_ref.shape[1]
    ff = jnp.zeros_like(x1)
    for c in range(d_ff // ff_chunk):
        hid = jnp.dot(x1b, wu_ref[:, c * ff_chunk:(c + 1) * ff_chunk], preferred_element_type=F32)
        hid = jnp.maximum(hid, 0.0)
        hid = (hid * hid).astype(BF16)
        ff = ff + jnp.dot(hid, wd_ref[c * ff_chunk:(c + 1) * ff_chunk, :], preferred_element_type=F32)
    y_ref[...] = _layer_norm(alpha * x1 + ff, g2_ref[...], b2_ref[...])


def _post(x2d, oa, ob, wo, g1, b1, wu, wd, g2, b2, alpha):
    R, D = x2d.shape
    tm = ROW_TILE
    d_ff = wu.shape[1]
    row = lambda i: (i, 0)
    const = lambda i: (0, 0)
    return pl.pallas_call(
        functools.partial(_post_kernel, alpha=alpha, ff_chunk=1024),
        grid=(R // tm,),
        in_specs=[pl.BlockSpec((tm, D), row),
                  pl.BlockSpec((tm, GROUP_W), row),
                  pl.BlockSpec((tm, GROUP_W), row),
                  pl.BlockSpec((2 * GROUP_W, D), const),
                  pl.BlockSpec((1, D), const), pl.BlockSpec((1, D), const),
                  pl.BlockSpec((D, d_ff), const),
                  pl.BlockSpec((d_ff, D), const),
                  pl.BlockSpec((1, D), const), pl.BlockSpec((1, D), const)],
        out_specs=pl.BlockSpec((tm, D), row),
        out_shape=jax.ShapeDtypeStruct((R, D), F32),
        compiler_params=_cparams(1),
        name="post_sublayers",
    )(x2d, oa, ob, wo, g1, b1, wu, wd, g2, b2)


def _rope_tables(pos):
    half = HEAD_DIM // 2
    inv = ROPE_THETA ** (-jnp.arange(half, dtype=F32) / half)
    ang = pos.astype(F32)[:, None] * inv[None, :]
    cos, sin = jnp.cos(ang), jnp.sin(ang)
    cos_h = jnp.concatenate([cos, cos], axis=1)
    sin_h = jnp.concatenate([-sin, sin], axis=1)
    return jnp.tile(cos_h, (1, HEADS_PER_SLAB)), jnp.tile(sin_h, (1, HEADS_PER_SLAB))


def _band_bias(table, group_chunks):
    gq = group_chunks * CHUNK
    win = A_PAST + gq
    d_min, d_max = A_PAST - (win - 1), gq - 1 + A_PAST
    n = d_max - d_min + 1
    left = jnp.repeat(table[:, :1], max(0, -REL_CLIP - d_min), axis=1)
    mid = table[:, max(d_min, -REL_CLIP) + REL_CLIP:min(d_max, REL_CLIP) + REL_CLIP + 1]
    right = jnp.repeat(table[:, -1:], max(0, d_max - REL_CLIP), axis=1)
    ext = jnp.concatenate([left, mid, right], axis=1).astype(F32)
    rp = jnp.concatenate([ext[:, ::-1], jnp.zeros((N_HEADS, 1), F32)], axis=1)
    shifted = jnp.tile(rp, (1, gq))[:, :gq * n].reshape(N_HEADS, gq, n)
    bias = shifted[:, :, gq - 1:gq - 1 + win]
    q_sub = jnp.arange(gq)[:, None] // CHUNK
    k_rel = jnp.arange(win)[None, :] - q_sub * CHUNK
    in_band = (k_rel >= 0) & (k_rel < A_BAND)
    bias = jnp.where(in_band[None], bias, -jnp.inf)
    return bias.reshape(N_SLABS, HEADS_PER_SLAB * gq, win)


def _dup_lanes(ki):
    return jnp.concatenate([ki, ki], axis=-1)


def kernel(x_prompt, x_sample, cache_a_k, cache_a_v, cache_b_k, cache_b_v, cache_b_kidx,
           w_in, rel_bias, w_out, ln1_g, ln1_b, w_up, w_down, ln2_g, ln2_b):
    B, S, D = x_prompt.shape
    Bs, T, _ = x_sample.shape
    depth = w_in.shape[0]
    p_len = cache_b_k.shape[2]
    a_len = cache_a_k.shape[2]
    assert S % ROW_TILE == 0 and (Bs * T) % ROW_TILE == 0 and T == CHUNK and a_len == A_PAST
    alpha = float((2 * depth) ** 0.25)
    topk_p = min(TOPK_MAX, S // 4)
    topk_s = min(TOPK_MAX, (p_len + T) // 4)
    keep_a = min(A_PAST, S)
    assert keep_a == ROW_TILE
    tiles_per_seq = S // ROW_TILE

    cos_p, sin_p = _rope_tables(jnp.arange(S))
    pos_s = jnp.tile(p_len + jnp.arange(T), ROW_TILE // T)
    cos_s, sin_s = _rope_tables(pos_s)

    tq_p = 2 * CHUNK
    widths_p = tuple(range(SPARSE_WIDTH_STEP, S + 1, SPARSE_WIDTH_STEP))
    lk_s = -(-(p_len + T) // LANES) * LANES
    pad_s = lk_s - (p_len + T)

    xp = x_prompt.reshape(B * S, D)
    xs = x_sample.reshape(Bs * T, D)
    outs = {k: [] for k in ("ak_s", "av_s", "bk_s", "bv_s", "bi_s")}
    carried = None

    for l in range(depth):
        w = jnp.pad(w_in[l], ((0, 0), (0, IN_PAD - w_in.shape[2]))).astype(BF16)
        wo = w_out[l].astype(BF16)
        wu = w_up[l].astype(BF16)
        wd = w_down[l].astype(BF16)
        g1, b1 = ln1_g[l][None, :], ln1_b[l][None, :]
        g2, b2 = ln2_g[l][None, :], ln2_b[l][None, :]
        bias_p = _band_bias(rel_bias[l], BAND_GROUP_CHUNKS)
        bias_s = _band_bias(rel_bias[l], 1)

        (qa, ka, va, qb, kb, vb, qi, ki2, *cache_p) = _project(xp, w, cos_p, sin_p, tiles_per_seq,
                                                                stacked=(l, depth, B, carried))
        small = cache_p[4]
        carried = cache_p[:4] + cache_p[5:]
        r3 = lambda a: a.reshape(B, S, a.shape[-1])
        oa = _band_attention(r3(qa), r3(ka), r3(va), bias_p, q_off=0, group_chunks=BAND_GROUP_CHUNKS)
        ob = _sparse_attention(r3(qb), r3(kb), r3(vb), r3(qi), r3(ki2), r3(small), tq=tq_p,
                               widths=widths_p, causal_blocks=True, q_pos_base=0, topk=topk_p)
        xp = _post(xp, oa.reshape(B * S, GROUP_W), ob.reshape(B * S, GROUP_W), wo, g1, b1, wu, wd, g2, b2, alpha)

        (qa, ka, va, qb, kb, vb, qi, ki2, kaf, vaf, kbf, vbf, small) = _project(xs, w, cos_s, sin_s, 1)
        s3 = lambda a: a.reshape(Bs, T, a.shape[-1])
        cat_a = lambda c, n: jnp.concatenate([c.reshape(Bs, a_len, GROUP_W).astype(BF16), s3(n)], axis=1)
        oa = _band_attention(s3(qa), cat_a(cache_a_k[l], ka), cat_a(cache_a_v[l], va), bias_s,
                             q_off=a_len, group_chunks=1)
        zpad = lambda n: jnp.zeros((Bs, pad_s, n), BF16)
        cat_b = lambda c, n: jnp.concatenate(
            [c.reshape(Bs, p_len, GROUP_W).astype(BF16), s3(n), zpad(GROUP_W)], axis=1)
        ki2_all = jnp.concatenate([_dup_lanes(cache_b_kidx[l]).astype(BF16), s3(ki2), zpad(LANES)], axis=1)
        ob = _sparse_attention(s3(qb), cat_b(cache_b_k[l], kb), cat_b(cache_b_v[l], vb), s3(qi), ki2_all,
                               s3(small), tq=T, widths=(lk_s,), causal_blocks=False, q_pos_base=p_len,
                               topk=topk_s)
        xs = _post(xs, oa.reshape(Bs * T, GROUP_W), ob.reshape(Bs * T, GROUP_W), wo, g1, b1, wu, wd, g2, b2, alpha)
        outs["ak_s"].append(kaf.reshape(Bs, T, N_HEADS, HEAD_DIM))
        outs["av_s"].append(vaf.reshape(Bs, T, N_HEADS, HEAD_DIM))
        outs["bk_s"].append(kbf.reshape(Bs, T, N_HEADS, HEAD_DIM))
        outs["bv_s"].append(vbf.reshape(Bs, T, N_HEADS, HEAD_DIM))
        outs["bi_s"].append(small[:, :IDX_DIM].reshape(Bs, T, IDX_DIM))

    ak_t, av_t, bk_t, bv_t, bi_t = carried
    heads_last = lambda a: a.reshape(depth, B, N_HEADS, HEAD_DIM, a.shape[-1]).transpose(0, 1, 4, 2, 3)
    st = lambda k: jnp.stack(outs[k])
    return (xp.reshape(B, S, D), xs.reshape(Bs, T, D),
            heads_last(ak_t), heads_last(av_t), heads_last(bk_t), heads_last(bv_t), bi_t.transpose(0, 1, 3, 2),
            st("ak_s"), st("av_s"), st("bk_s"), st("bv_s"), st("bi_s"))
```

```python
import functools

import jax
import jax.numpy as jnp
import numpy as np
from jax import lax
from jax.experimental import pallas as pl
from jax.experimental.pallas import tpu as pltpu

F32 = jnp.float32
BF16 = jnp.bfloat16
I32 = jnp.int32
I16 = jnp.int16

CHUNK = 64
HEAD_DIM = 64
N_HEADS = 8
GROUP_W = N_HEADS * HEAD_DIM
A_PAST = 8 * CHUNK
A_BAND = A_PAST + CHUNK
REL_CLIP = 128
IDX_HEADS = 8
IDX_DIM = 64
TOPK_MAX = 256
ROPE_THETA = 10000.0
LN_EPS = 1e-5

LANES = 128
PACKED_ROWS = 16
HEADS_PER_SLAB = LANES // HEAD_DIM
N_SLABS = GROUP_W // LANES
ROW_TILE = 512
BAND_GROUP_CHUNKS = 4
SEARCH_BITS = 16
COUNT_ACCUMULATORS = 4
SPARSE_WIDTH_STEP = 512
VMEM_LIMIT = 56 * 1024 * 1024

IN_MAIN = 7 * GROUP_W
IN_SMALL0 = IN_MAIN
IN_PAD = IN_MAIN + LANES

KEY_NEG_INF = np.int32(np.int32(np.float32(-np.inf).view(np.int32)) ^ np.int32(0x7FFFFFFF))


def _cparams(n_axes):
    return pltpu.CompilerParams(dimension_semantics=("arbitrary",) * n_axes,
                                vmem_limit_bytes=VMEM_LIMIT)


def _proj_kernel(*refs, keep_tiles, transposed, n_alias):
    x_ref, w_ref, cos_ref, sin_ref = refs[:4]
    outs = refs[4 + n_alias:]
    qa_o, ka_o, va_o, qb_o, kb_o, vb_o, qi_o, ki2_o, kaf_o, vaf_o, kbf_o, vbf_o, small_o = outs[:13]
    i = pl.program_id(0)
    tm = x_ref.shape[0]
    x = x_ref[...].astype(BF16)
    cos = cos_ref[...]
    sin = sin_ref[...]
    lane = lax.broadcasted_iota(I32, (tm, LANES), 1)
    first_half = (lane & (HEAD_DIM - 1)) < (HEAD_DIM // 2)

    def sec(c0, n):
        return jnp.dot(x, w_ref[:, c0:c0 + n], preferred_element_type=F32)

    def rope_slab(slab):
        swapped = jnp.where(first_half, pltpu.roll(slab, LANES - HEAD_DIM // 2, 1),
                            pltpu.roll(slab, HEAD_DIM // 2, 1))
        return slab * cos + swapped * sin

    def rope(h):
        return jnp.concatenate([rope_slab(h[:, s * LANES:(s + 1) * LANES])
                                for s in range(h.shape[1] // LANES)], axis=1)

    def put_cache(o_ref, val):
        if transposed:
            o_ref[0, 0] = val.T
        else:
            o_ref[...] = val

    last = (i % keep_tiles) == (keep_tiles - 1)

    qa_o[...] = (sec(0 * GROUP_W, GROUP_W) * (HEAD_DIM ** -0.5)).astype(BF16)
    ka = sec(1 * GROUP_W, GROUP_W)
    ka_o[...] = ka.astype(BF16)
    va = sec(2 * GROUP_W, GROUP_W)
    va_o[...] = va.astype(BF16)

    @pl.when(last)
    def _():
        put_cache(kaf_o, ka)
        put_cache(vaf_o, va)

    qb_o[...] = (rope(sec(3 * GROUP_W, GROUP_W)) * (HEAD_DIM ** -0.5)).astype(BF16)
    kb = rope(sec(4 * GROUP_W, GROUP_W))
    put_cache(kbf_o, kb)
    kb_o[...] = kb.astype(BF16)
    vb = sec(5 * GROUP_W, GROUP_W)
    put_cache(vbf_o, vb)
    vb_o[...] = vb.astype(BF16)
    qi_o[...] = rope(sec(6 * GROUP_W, GROUP_W)).astype(BF16)

    sm = sec(IN_SMALL0, LANES)
    ki = rope_slab(sm)
    is_ki = lane < IDX_DIM
    small_o[...] = jnp.where(is_ki, ki, sm * ((IDX_HEADS * IDX_DIM) ** -0.5))
    ki2_o[...] = jnp.where(is_ki, ki, pltpu.roll(ki, IDX_DIM, 1)).astype(BF16)
    if transposed:
        outs[13][0, 0] = ki.T[:IDX_DIM, :]


def _project(x2d, w, cos_t, sin_t, keep_tiles, stacked=None):
    R, D = x2d.shape
    tm = ROW_TILE
    n_tiles = R // tm
    n_pos = cos_t.shape[0] // tm
    row = lambda i: (i, 0)
    bf = lambda n: jax.ShapeDtypeStruct((R, n), BF16)
    ff = lambda r, n: jax.ShapeDtypeStruct((r, n), F32)
    blk = lambda n: pl.BlockSpec((tm, n), row)
    out_shape = [bf(GROUP_W)] * 7 + [bf(LANES)]
    out_specs = [blk(GROUP_W)] * 7 + [blk(LANES)]
    args = [x2d, w, cos_t, sin_t]
    in_specs = [pl.BlockSpec((tm, D), row),
                pl.BlockSpec((D, IN_PAD), lambda i: (0, 0)),
                pl.BlockSpec((tm, LANES), lambda i: (i % n_pos, 0)),
                pl.BlockSpec((tm, LANES), lambda i: (i % n_pos, 0))]
    aliases = {}
    if stacked is None:
        keep_blk = pl.BlockSpec((tm, GROUP_W), lambda i: (i // keep_tiles, 0))
        out_shape += [ff(R // keep_tiles, GROUP_W), ff(R // keep_tiles, GROUP_W),
                      ff(R, GROUP_W), ff(R, GROUP_W), ff(R, LANES)]
        out_specs += [keep_blk, keep_blk, blk(GROUP_W), blk(GROUP_W), blk(LANES)]
    else:
        layer, depth, batch, carried = stacked
        seq = keep_tiles * tm
        tsd = lambda n, t: jax.ShapeDtypeStruct((depth, batch, n, t), F32)
        kept = lambda n: pl.BlockSpec((1, 1, n, tm), lambda i: (layer, i // keep_tiles, 0, 0))
        full = lambda n: pl.BlockSpec((1, 1, n, tm), lambda i: (layer, i // keep_tiles, 0, i % keep_tiles))
        out_shape += [tsd(GROUP_W, tm), tsd(GROUP_W, tm), tsd(GROUP_W, seq), tsd(GROUP_W, seq),
                      ff(R, LANES), tsd(IDX_DIM, seq)]
        out_specs += [kept(GROUP_W), kept(GROUP_W), full(GROUP_W), full(GROUP_W), blk(LANES), full(IDX_DIM)]
        if carried is not None:
            carried_out = (8, 9, 10, 11, 13)
            aliases = {len(args) + k: o for k, o in enumerate(carried_out)}
            args += list(carried)
            in_specs += [pl.BlockSpec(memory_space=pl.ANY)] * len(carried)
    return pl.pallas_call(
        functools.partial(_proj_kernel, keep_tiles=keep_tiles, transposed=stacked is not None,
                          n_alias=len(aliases)),
        grid=(n_tiles,),
        in_specs=in_specs,
        out_specs=out_specs,
        out_shape=out_shape,
        input_output_aliases=aliases,
        compiler_params=_cparams(1),
        name="proj_rope",
    )(*args)


def _head_masks(rows):
    lane = lax.broadcasted_iota(I32, (rows, LANES), 1)
    return [jnp.where((lane >= h * HEAD_DIM) & (lane < (h + 1) * HEAD_DIM), 1.0, 0.0).astype(BF16)
            for h in range(HEADS_PER_SLAB)]


def _band_kernel(q_ref, k_ref, v_ref, bias_ref, o_ref, *, n_groups, group_chunks, q_off):
    gq = group_chunks * CHUNK
    win = A_PAST + gq
    lane = lax.broadcasted_iota(I32, (gq, LANES), 1)
    head_masks = _head_masks(gq)
    for gi in range(n_groups):
        qs = gi * gq
        start = q_off + qs - A_PAST
        lo = max(start, 0)
        hi = q_off + qs + gq
        q2 = q_ref[0, qs:qs + gq, :]
        kw = k_ref[0, lo:hi, :]
        vw = v_ref[0, lo:hi, :]
        qst = jnp.concatenate([q2 * head_masks[h] for h in range(HEADS_PER_SLAB)], axis=0)
        s = lax.dot_general(qst, kw, (((1,), (1,)), ((), ())), preferred_element_type=F32)
        s = s + bias_ref[0, :, lo - start:win]
        m = jnp.max(s, axis=1, keepdims=True)
        p = jnp.exp(s - m)
        l = jnp.sum(p, axis=1, keepdims=True)
        o = jnp.dot(p.astype(BF16), vw, preferred_element_type=F32) / l
        o2 = jnp.where(lane < HEAD_DIM, o[:gq], o[gq:])
        o_ref[0, qs:qs + gq, :] = o2.astype(BF16)


def _band_attention(q, k, v, bias, q_off, group_chunks):
    B, Sq, _ = q.shape
    Sk = k.shape[1]
    gq = group_chunks * CHUNK
    return pl.pallas_call(
        functools.partial(_band_kernel, n_groups=Sq // gq, group_chunks=group_chunks, q_off=q_off),
        grid=(B, N_SLABS),
        in_specs=[pl.BlockSpec((1, Sq, LANES), lambda b, g: (b, 0, g)),
                  pl.BlockSpec((1, Sk, LANES), lambda b, g: (b, 0, g)),
                  pl.BlockSpec((1, Sk, LANES), lambda b, g: (b, 0, g)),
                  pl.BlockSpec((1, HEADS_PER_SLAB * gq, A_PAST + gq), lambda b, g: (g, 0, 0))],
        out_specs=pl.BlockSpec((1, Sq, LANES), lambda b, g: (b, 0, g)),
        out_shape=jax.ShapeDtypeStruct((B, Sq, GROUP_W), BF16),
        compiler_params=_cparams(2),
        name="band_attention",
    )(q, k, v, bias)


def _key_tiles(a, axis):
    step = LANES if axis == 1 else PACKED_ROWS
    n = a.shape[axis] // step
    return [a[:, t * step:(t + 1) * step] if axis == 1 else a[t * step:(t + 1) * step, :] for t in range(n)]


def _per_query(a, axis):
    return jnp.broadcast_to(a, (a.shape[0], LANES) if axis == 1 else (PACKED_ROWS, a.shape[1]))


def _count16(vals, cands, strict, axis):
    c16s = [_per_query(c, axis).astype(I16) for c in cands]
    tiles = _key_tiles(vals, axis)
    assert len(tiles) < 2 ** 8
    accs = [[None] * COUNT_ACCUMULATORS for _ in cands]
    for t, v in enumerate(tiles):
        for acc, c16 in zip(accs, c16s):
            hit = jnp.where((v > c16) if strict else (v >= c16), jnp.ones((), BF16), jnp.zeros((), BF16))
            slot = t % COUNT_ACCUMULATORS
            acc[slot] = hit if acc[slot] is None else acc[slot] + hit
    totals = [functools.reduce(lambda a, b: a + b, [a for a in acc if a is not None]) for acc in accs]
    return [jnp.sum(tot.astype(F32), axis=axis, keepdims=True) for tot in totals]


def _kth_largest16(vals, need, axis):
    bias = np.int32(2 ** 15)
    n_rounds = SEARCH_BITS // 2

    def search_round(it, u):
        b0 = np.int32(1) << (np.int32(2) * (np.int32(n_rounds - 1) - it))
        b1 = b0 + b0
        ok1, ok3, ok0 = (n >= need for n in
                         _count16(vals, [u + c - bias for c in (b1, b1 + b0, b0)], False, axis))
        return jnp.where(ok1, jnp.where(ok3, u + (b1 + b0), u + b1), jnp.where(ok0, u + b0, u))

    return lax.fori_loop(0, n_rounds, search_round, jnp.zeros(need.shape, I32)) - bias


def _kth_largest_key(key, kf, axis):
    hi = (key >> 16).astype(I16)
    lo = ((key & np.int32(0xFFFF)) - np.int32(2 ** 15)).astype(I16)
    need_hi = jnp.full((key.shape[0], 1) if axis == 1 else (1, key.shape[1]), kf, F32)
    p_hi = _kth_largest16(hi, need_hi, axis)
    need_lo = kf - _count16(hi, [p_hi], True, axis)[0]
    p16 = _per_query(p_hi, axis).astype(I16)
    lo_in_bucket = jnp.concatenate(
        [jnp.where(h == p16, l, np.int16(-2 ** 15)) for h, l in zip(_key_tiles(hi, axis), _key_tiles(lo, axis))],
        axis=axis)
    p_lo = _kth_largest16(lo_in_bucket, need_lo, axis)
    return (p_hi << 16) + (p_lo + np.int32(2 ** 15))


def _admissible(tq, W, q_pos0, keys_on_rows=False):
    shape, q_ax, k_ax = ((W, tq), 1, 0) if keys_on_rows else ((tq, W), 0, 1)
    q_idx = lax.broadcasted_iota(I32, shape, q_ax)
    k_idx = lax.broadcasted_iota(I32, shape, k_ax)
    chunk_shift = CHUNK.bit_length() - 1
    return (k_idx >> chunk_shift) <= ((q_idx + q_pos0) >> chunk_shift)


def _topk_negmask(qi_ref, ki2_ref, small_ref, *, width, q_pos0, topk, keys_on_rows):
    tq = qi_ref.shape[1]
    W = width
    axis = 0 if keys_on_rows else 1
    head_masks = _head_masks(tq)

    small = small_ref[0]
    qi = qi_ref[0]
    ki2 = ki2_ref[0, :W, :]
    masked = [qi[:, g * LANES:(g + 1) * LANES] * head_masks[h]
              for g in range(N_SLABS) for h in range(HEADS_PER_SLAB)]
    if keys_on_rows:
        dots = jnp.dot(ki2, jnp.concatenate([m.T for m in masked], axis=1), preferred_element_type=F32)
        weights = small.T
        head_dots = lambda h: dots[:, h * tq:(h + 1) * tq]
        head_w = lambda h: weights[IDX_DIM + h:IDX_DIM + h + 1, :]
    else:
        dots = lax.dot_general(jnp.concatenate(masked, axis=0), ki2, (((1,), (1,)), ((), ())),
                               preferred_element_type=F32)
        head_dots = lambda h: dots[h * tq:(h + 1) * tq, :]
        head_w = lambda h: small[:, IDX_DIM + h:IDX_DIM + h + 1]
    score = head_w(0) * jnp.maximum(head_dots(0), 0.0)
    for h in range(1, IDX_HEADS):
        score = score + head_w(h) * jnp.maximum(head_dots(h), 0.0)

    adm = _admissible(tq, W, q_pos0, keys_on_rows)
    score = jnp.where(adm, score + 0.0, -jnp.inf)

    bits = pltpu.bitcast(score, I32)
    key = bits ^ ((bits >> 31) & np.int32(0x7FFFFFFF))
    kf = np.float32(topk)
    thr = _kth_largest_key(key, kf, axis)

    thr = jnp.maximum(thr, KEY_NEG_INF + np.int32(1))
    ge = key >= thr
    n_ge = jnp.sum(jnp.where(ge, 1.0, 0.0), axis=axis, keepdims=True)
    sel_simple = jnp.where(ge, 0.0, -jnp.inf)

    def tie_path():
        gt = key > thr
        eq = key == thr
        need = kf - jnp.sum(jnp.where(gt, 1.0, 0.0), axis=axis, keepdims=True)
        eqf = jnp.where(eq, 1.0, 0.0).astype(BF16)
        ui = lax.broadcasted_iota(I32, (LANES, LANES), 0)
        uj = lax.broadcasted_iota(I32, (LANES, LANES), 1)
        tri = jnp.where((uj < ui) if keys_on_rows else (ui < uj), 1.0, 0.0).astype(BF16)
        offset = jnp.zeros_like(need)
        pieces = []
        for t in range(W // LANES):
            if keys_on_rows:
                e = eqf[t * LANES:(t + 1) * LANES, :]
                within = jnp.dot(tri, e, preferred_element_type=F32)
            else:
                e = eqf[:, t * LANES:(t + 1) * LANES]
                within = jnp.dot(e, tri, preferred_element_type=F32)
            pieces.append(offset + within < need)
            offset = offset + jnp.sum(e.astype(F32), axis=axis, keepdims=True)
        tie_ok = jnp.concatenate(pieces, axis=axis)
        return jnp.where(gt | (eq & tie_ok), 0.0, -jnp.inf)

    any_excess = jnp.max(n_ge) > kf
    negmask = lax.cond(any_excess, tie_path, lambda: sel_simple)
    return negmask.T if keys_on_rows else negmask


def _masked_attention(qb_ref, kb_ref, vb_ref, o_ref, negmask, *, width):
    tq = qb_ref.shape[1]
    W = width
    lane = lax.broadcasted_iota(I32, (tq, LANES), 1)
    head_masks = _head_masks(tq)
    qb = qb_ref[0]
    mask2 = jnp.concatenate([negmask] * HEADS_PER_SLAB, axis=0)
    for g in range(N_SLABS):
        q2 = qb[:, g * LANES:(g + 1) * LANES]
        k2 = kb_ref[0, :W, g * LANES:(g + 1) * LANES]
        v2 = vb_ref[0, :W, g * LANES:(g + 1) * LANES]
        qst = jnp.concatenate([q2 * head_masks[h] for h in range(HEADS_PER_SLAB)], axis=0)
        s = lax.dot_general(qst, k2, (((1,), (1,)), ((), ())), preferred_element_type=F32) + mask2
        m = jnp.max(s, axis=1, keepdims=True)
        p = jnp.exp(s - m)
        l = jnp.sum(p, axis=1, keepdims=True)
        o = jnp.dot(p.astype(BF16), v2, preferred_element_type=F32) / l
        o2 = jnp.where(lane < HEAD_DIM, o[:tq], o[tq:])
        o_ref[0, :, g * LANES:(g + 1) * LANES] = o2.astype(BF16)


def _sparse_kernel(qb_ref, kb_ref, vb_ref, qi_ref, ki2_ref, small_ref, o_ref, *,
                   widths, causal_blocks, q_pos_base, topk):
    tq = qb_ref.shape[1]
    j = pl.program_id(1)

    def body(W, q_pos0):
        negmask = _topk_negmask(qi_ref, ki2_ref, small_ref, width=W, q_pos0=q_pos0, topk=topk,
                                keys_on_rows=(tq % LANES == 0))
        _masked_attention(qb_ref, kb_ref, vb_ref, o_ref, negmask, width=W)

    if not causal_blocks:
        body(widths[0], q_pos_base)
        return
    q_pos0 = j * tq
    n_keys = (j + 1) * tq
    dense_w = -(-topk // LANES) * LANES

    @pl.when(n_keys <= topk)
    def _():
        negmask = jnp.where(_admissible(tq, dense_w, q_pos0), 0.0, -jnp.inf)
        _masked_attention(qb_ref, kb_ref, vb_ref, o_ref, negmask, width=dense_w)

    lo = topk
    for W in widths:
        if W <= lo:
            continue

        @pl.when((n_keys > lo) & (n_keys <= W))
        def _(W=W):
            body(W, q_pos0)
        lo = max(W, topk)


def _sparse_attention(qb, kb, vb, qi, ki2, small, *, tq, widths, causal_blocks, q_pos_base, topk):
    B, Sq, _ = qb.shape
    Lk = kb.shape[1]
    nq = Sq // tq
    qspec = lambda n: pl.BlockSpec((1, tq, n), lambda b, j: (b, j, 0))
    kspec = lambda n: pl.BlockSpec((1, Lk, n), lambda b, j: (b, 0, 0))
    return pl.pallas_call(
        functools.partial(_sparse_kernel, widths=widths, causal_blocks=causal_blocks,
                          q_pos_base=q_pos_base, topk=topk),
        grid=(B, nq),
        in_specs=[qspec(GROUP_W), kspec(GROUP_W), kspec(GROUP_W), qspec(GROUP_W), kspec(LANES), qspec(LANES)],
        out_specs=qspec(GROUP_W),
        out_shape=jax.ShapeDtypeStruct((B, Sq, GROUP_W), BF16),
        compiler_params=_cparams(2),
        name="sparse_attention",
    )(qb, kb, vb, qi, ki2, small)


def _layer_norm(x, g, b):
    mu = jnp.mean(x, axis=1, keepdims=True)
    xc = x - mu
    var = jnp.mean(xc * xc, axis=1, keepdims=True)
    return xc * lax.rsqrt(var + LN_EPS) * g + b


def _post_kernel(x_ref, oa_ref, ob_ref, wo_ref, g1_ref, b1_ref, wu_ref, wd_ref, g2_ref, b2_ref, y_ref,
                 *, alpha, ff_chunk):
    x = x_ref[...]
    mix = jnp.dot(oa_ref[...], wo_ref[:GROUP_W, :], preferred_element_type=F32)
    mix = mix + jnp.dot(ob_ref[...], wo_ref[GROUP_W:, :], preferred_element_type=F32)
    x1 = _layer_norm(alpha * x + mix, g1_ref[...], b1_ref[...])
    x1b = x1.astype(BF16)
    d_ff = wu_ref.shape[1]
    ff = jnp.zeros_like(x1)
    for c in range(d_ff // ff_chunk):
        hid = jnp.dot(x1b, wu_ref[:, c * ff_chunk:(c + 1) * ff_chunk], preferred_element_type=F32)
        hid = jnp.maximum(hid, 0.0)
        hid = (hid * hid).astype(BF16)
        ff = ff + jnp.dot(hid, wd_ref[c * ff_chunk:(c + 1) * ff_chunk, :], preferred_element_type=F32)
    y_ref[...] = _layer_norm(alpha * x1 + ff, g2_ref[...], b2_ref[...])


def _post(x2d, oa, ob, wo, g1, b1, wu, wd, g2, b2, alpha):
    R, D = x2d.shape
    tm = ROW_TILE
    d_ff = wu.shape[1]
    row = lambda i: (i, 0)
    const = lambda i: (0, 0)
    return pl.pallas_call(
        functools.partial(_post_kernel, alpha=alpha, ff_chunk=1024),
        grid=(R // tm,),
        in_specs=[pl.BlockSpec((tm, D), row),
                  pl.BlockSpec((tm, GROUP_W), row),
                  pl.BlockSpec((tm, GROUP_W), row),
                  pl.BlockSpec((2 * GROUP_W, D), const),
                  pl.BlockSpec((1, D), const), pl.BlockSpec((1, D), const),
                  pl.BlockSpec((D, d_ff), const),
                  pl.BlockSpec((d_ff, D), const),
                  pl.BlockSpec((1, D), const), pl.BlockSpec((1, D), const)],
        out_specs=pl.BlockSpec((tm, D), row),
        out_shape=jax.ShapeDtypeStruct((R, D), F32),
        compiler_params=_cparams(1),
        name="post_sublayers",
    )(x2d, oa, ob, wo, g1, b1, wu, wd, g2, b2)


def _rope_tables(pos):
    half = HEAD_DIM // 2
    inv = ROPE_THETA ** (-jnp.arange(half, dtype=F32) / half)
    ang = pos.astype(F32)[:, None] * inv[None, :]
    cos, sin = jnp.cos(ang), jnp.sin(ang)
    cos_h = jnp.concatenate([cos, cos], axis=1)
    sin_h = jnp.concatenate([-sin, sin], axis=1)
    return jnp.tile(cos_h, (1, HEADS_PER_SLAB)), jnp.tile(sin_h, (1, HEADS_PER_SLAB))


def _band_bias(table, group_chunks):
    gq = group_chunks * CHUNK
    win = A_PAST + gq
    d_min, d_max = A_PAST - (win - 1), gq - 1 + A_PAST
    n = d_max - d_min + 1
    left = jnp.repeat(table[:, :1], max(0, -REL_CLIP - d_min), axis=1)
    mid = table[:, max(d_min, -REL_CLIP) + REL_CLIP:min(d_max, REL_CLIP) + REL_CLIP + 1]
    right = jnp.repeat(table[:, -1:], max(0, d_max - REL_CLIP), axis=1)
    ext = jnp.concatenate([left, mid, right], axis=1).astype(F32)
    rp = jnp.concatenate([ext[:, ::-1], jnp.zeros((N_HEADS, 1), F32)], axis=1)
    shifted = jnp.tile(rp, (1, gq))[:, :gq * n].reshape(N_HEADS, gq, n)
    bias = shifted[:, :, gq - 1:gq - 1 + win]
    q_sub = jnp.arange(gq)[:, None] // CHUNK
    k_rel = jnp.arange(win)[None, :] - q_sub * CHUNK
    in_band = (k_rel >= 0) & (k_rel < A_BAND)
    bias = jnp.where(in_band[None], bias, -jnp.inf)
    return bias.reshape(N_SLABS, HEADS_PER_SLAB * gq, win)


def _dup_lanes(ki):
    return jnp.concatenate([ki, ki], axis=-1)


def kernel(x_prompt, x_sample, cache_a_k, cache_a_v, cache_b_k, cache_b_v, cache_b_kidx,
           w_in, rel_bias, w_out, ln1_g, ln1_b, w_up, w_down, ln2_g, ln2_b):
    B, S, D = x_prompt.shape
    Bs, T, _ = x_sample.shape
    depth = w_in.shape[0]
    p_len = cache_b_k.shape[2]
    a_len = cache_a_k.shape[2]
    assert S % ROW_TILE == 0 and (Bs * T) % ROW_TILE == 0 and T == CHUNK and a_len == A_PAST
    alpha = float((2 * depth) ** 0.25)
    topk_p = min(TOPK_MAX, S // 4)
    topk_s = min(TOPK_MAX, (p_len + T) // 4)
    keep_a = min(A_PAST, S)
    assert keep_a == ROW_TILE
    tiles_per_seq = S // ROW_TILE

    cos_p, sin_p = _rope_tables(jnp.arange(S))
    pos_s = jnp.tile(p_len + jnp.arange(T), ROW_TILE // T)
    cos_s, sin_s = _rope_tables(pos_s)

    tq_p = 2 * CHUNK
    widths_p = tuple(range(SPARSE_WIDTH_STEP, S + 1, SPARSE_WIDTH_STEP))
    lk_s = -(-(p_len + T) // LANES) * LANES
    pad_s = lk_s - (p_len + T)

    xp = x_prompt.reshape(B * S, D)
    xs = x_sample.reshape(Bs * T, D)
    outs = {k: [] for k in ("ak_s", "av_s", "bk_s", "bv_s", "bi_s")}
    carried = None

    for l in range(depth):
        w = jnp.pad(w_in[l], ((0, 0), (0, IN_PAD - w_in.shape[2]))).astype(BF16)
        wo = w_out[l].astype(BF16)
        wu = w_up[l].astype(BF16)
        wd = w_down[l].astype(BF16)
        g1, b1 = ln1_g[l][None, :], ln1_b[l][None, :]
        g2, b2 = ln2_g[l][None, :], ln2_b[l][None, :]
        bias_p = _band_bias(rel_bias[l], BAND_GROUP_CHUNKS)
        bias_s = _band_bias(rel_bias[l], 1)

        (qa, ka, va, qb, kb, vb, qi, ki2, *cache_p) = _project(xp, w, cos_p, sin_p, tiles_per_seq,
                                                                stacked=(l, depth, B, carried))
        small = cache_p[4]
        carried = cache_p[:4] + cache_p[5:]
        r3 = lambda a: a.reshape(B, S, a.shape[-1])
        oa = _band_attention(r3(qa), r3(ka), r3(va), bias_p, q_off=0, group_chunks=BAND_GROUP_CHUNKS)
        ob = _sparse_attention(r3(qb), r3(kb), r3(vb), r3(qi), r3(ki2), r3(small), tq=tq_p,
                               widths=widths_p, causal_blocks=True, q_pos_base=0, topk=topk_p)
        xp = _post(xp, oa.reshape(B * S, GROUP_W), ob.reshape(B * S, GROUP_W), wo, g1, b1, wu, wd, g2, b2, alpha)

        (qa, ka, va, qb, kb, vb, qi, ki2, kaf, vaf, kbf, vbf, small) = _project(xs, w, cos_s, sin_s, 1)
        s3 = lambda a: a.reshape(Bs, T, a.shape[-1])
        cat_a = lambda c, n: jnp.concatenate([c.reshape(Bs, a_len, GROUP_W).astype(BF16), s3(n)], axis=1)
        oa = _band_attention(s3(qa), cat_a(cache_a_k[l], ka), cat_a(cache_a_v[l], va), bias_s,
                             q_off=a_len, group_chunks=1)
        zpad = lambda n: jnp.zeros((Bs, pad_s, n), BF16)
        cat_b = lambda c, n: jnp.concatenate(
            [c.reshape(Bs, p_len, GROUP_W).astype(BF16), s3(n), zpad(GROUP_W)], axis=1)
        ki2_all = jnp.concatenate([_dup_lanes(cache_b_kidx[l]).astype(BF16), s3(ki2), zpad(LANES)], axis=1)
        ob = _sparse_attention(s3(qb), cat_b(cache_b_k[l], kb), cat_b(cache_b_v[l], vb), s3(qi), ki2_all,
                               s3(small), tq=T, widths=(lk_s,), causal_blocks=False, q_pos_base=p_len,
                               topk=topk_s)
        xs = _post(xs, oa.reshape(Bs * T, GROUP_W), ob.reshape(Bs * T, GROUP_W), wo, g1, b1, wu, wd, g2, b2, alpha)
        outs["ak_s"].append(kaf.reshape(Bs, T, N_HEADS, HEAD_DIM))
        outs["av_s"].append(vaf.reshape(Bs, T, N_HEADS, HEAD_DIM))
        outs["bk_s"].append(kbf.reshape(Bs, T, N_HEADS, HEAD_DIM))
        outs["bv_s"].append(vbf.reshape(Bs, T, N_HEADS, HEAD_DIM))
        outs["bi_s"].append(small[:, :IDX_DIM].reshape(Bs, T, IDX_DIM))

    ak_t, av_t, bk_t, bv_t, bi_t = carried
    heads_last = lambda a: a.reshape(depth, B, N_HEADS, HEAD_DIM, a.shape[-1]).transpose(0, 1, 4, 2, 3)
    st = lambda k: jnp.stack(outs[k])
    return (xp.reshape(B, S, D), xs.reshape(Bs, T, D),
            heads_last(ak_t), heads_last(av_t), heads_last(bk_t), heads_last(bv_t), bi_t.transpose(0, 1, 3, 2),
            st("ak_s"), st("av_s"), st("bk_s"), st("bv_s"), st("bi_s"))
```

```python
import functools

import jax
import jax.numpy as jnp
import numpy as np
from jax import lax
from jax.experimental import pallas as pl
from jax.experimental.pallas import tpu as pltpu

F32 = jnp.float32
BF16 = jnp.bfloat16
I32 = jnp.int32
I16 = jnp.int16

CHUNK = 64
HEAD_DIM = 64
N_HEADS = 8
GROUP_W = N_HEADS * HEAD_DIM
A_PAST = 8 * CHUNK
A_BAND = A_PAST + CHUNK
REL_CLIP = 128
IDX_HEADS = 8
IDX_DIM = 64
TOPK_MAX = 256
ROPE_THETA = 10000.0
LN_EPS = 1e-5

LANES = 128
PACKED_ROWS = 16
HEADS_PER_SLAB = LANES // HEAD_DIM
N_SLABS = GROUP_W // LANES
ROW_TILE = 512
BAND_GROUP_CHUNKS = 4
SEARCH_BITS = 16
COUNT_ACCUMULATORS = 4
SPARSE_WIDTH_STEP = 256
VMEM_LIMIT = 56 * 1024 * 1024

IN_MAIN = 7 * GROUP_W
IN_SMALL0 = IN_MAIN
IN_PAD = IN_MAIN + LANES

KEY_NEG_INF = np.int32(np.int32(np.float32(-np.inf).view(np.int32)) ^ np.int32(0x7FFFFFFF))


def _cparams(n_axes):
    return pltpu.CompilerParams(dimension_semantics=("arbitrary",) * n_axes,
                                vmem_limit_bytes=VMEM_LIMIT)


def _proj_kernel(*refs, keep_tiles, transposed, n_alias):
    x_ref, w_ref, cos_ref, sin_ref = refs[:4]
    outs = refs[4 + n_alias:]
    qa_o, ka_o, va_o, qb_o, kb_o, vb_o, qi_o, ki2_o, kaf_o, vaf_o, kbf_o, vbf_o, small_o = outs[:13]
    i = pl.program_id(0)
    tm = x_ref.shape[0]
    x = x_ref[...].astype(BF16)
    cos = cos_ref[...]
    sin = sin_ref[...]
    lane = lax.broadcasted_iota(I32, (tm, LANES), 1)
    first_half = (lane & (HEAD_DIM - 1)) < (HEAD_DIM // 2)

    def sec(c0, n):
        return jnp.dot(x, w_ref[:, c0:c0 + n], preferred_element_type=F32)

    def rope_slab(slab):
        swapped = jnp.where(first_half, pltpu.roll(slab, LANES - HEAD_DIM // 2, 1),
                            pltpu.roll(slab, HEAD_DIM // 2, 1))
        return slab * cos + swapped * sin

    def rope(h):
        return jnp.concatenate([rope_slab(h[:, s * LANES:(s + 1) * LANES])
                                for s in range(h.shape[1] // LANES)], axis=1)

    def put_cache(o_ref, val):
        if transposed:
            o_ref[0, 0] = val.T
        else:
            o_ref[...] = val

    qa_o[...] = (sec(0 * GROUP_W, GROUP_W) * (HEAD_DIM ** -0.5)).astype(BF16)
    ka = sec(1 * GROUP_W, GROUP_W)
    ka_o[...] = ka.astype(BF16)
    va = sec(2 * GROUP_W, GROUP_W)
    va_o[...] = va.astype(BF16)

    put_cache(kaf_o, ka)
    put_cache(vaf_o, va)

    qb_o[...] = (rope(sec(3 * GROUP_W, GROUP_W)) * (HEAD_DIM ** -0.5)).astype(BF16)
    kb = rope(sec(4 * GROUP_W, GROUP_W))
    kb_o[...] = kb.astype(BF16)
    if transposed:
        kb_t = kb.T
        kbf_o[0, 0] = kb_t
        outs[14][0] = kb_t.astype(BF16)
    else:
        kbf_o[...] = kb
    vb = sec(5 * GROUP_W, GROUP_W)
    put_cache(vbf_o, vb)
    vb_o[...] = vb.astype(BF16)
    qi_o[...] = rope(sec(6 * GROUP_W, GROUP_W)).astype(BF16)

    sm = sec(IN_SMALL0, LANES)
    ki = rope_slab(sm)
    is_ki = lane < IDX_DIM
    small_o[...] = jnp.where(is_ki, ki, sm * ((IDX_HEADS * IDX_DIM) ** -0.5))
    ki2_o[...] = jnp.where(is_ki, ki, pltpu.roll(ki, IDX_DIM, 1)).astype(BF16)
    if transposed:
        outs[13][0, 0] = ki.T[:IDX_DIM, :]


def _project(x2d, w, cos_t, sin_t, keep_tiles, stacked=None):
    R, D = x2d.shape
    tm = ROW_TILE
    n_tiles = R // tm
    n_pos = cos_t.shape[0] // tm
    row = lambda i: (i, 0)
    bf = lambda n: jax.ShapeDtypeStruct((R, n), BF16)
    ff = lambda r, n: jax.ShapeDtypeStruct((r, n), F32)
    blk = lambda n: pl.BlockSpec((tm, n), row)
    out_shape = [bf(GROUP_W)] * 7 + [bf(LANES)]
    out_specs = [blk(GROUP_W)] * 7 + [blk(LANES)]
    args = [x2d, w, cos_t, sin_t]
    in_specs = [pl.BlockSpec((tm, D), row),
                pl.BlockSpec((D, IN_PAD), lambda i: (0, 0)),
                pl.BlockSpec((tm, LANES), lambda i: (i % n_pos, 0)),
                pl.BlockSpec((tm, LANES), lambda i: (i % n_pos, 0))]
    aliases = {}
    if stacked is None:
        keep_blk = pl.BlockSpec((tm, GROUP_W), lambda i: (i // keep_tiles, 0))
        out_shape += [ff(R // keep_tiles, GROUP_W), ff(R // keep_tiles, GROUP_W),
                      ff(R, GROUP_W), ff(R, GROUP_W), ff(R, LANES)]
        out_specs += [keep_blk, keep_blk, blk(GROUP_W), blk(GROUP_W), blk(LANES)]
    else:
        layer, depth, batch, carried = stacked
        seq = keep_tiles * tm
        tsd = lambda n, t: jax.ShapeDtypeStruct((depth, batch, n, t), F32)
        kept = lambda n: pl.BlockSpec((1, 1, n, tm), lambda i: (layer, i // keep_tiles, 0, 0))
        full = lambda n: pl.BlockSpec((1, 1, n, tm), lambda i: (layer, i // keep_tiles, 0, i % keep_tiles))
        out_shape += [tsd(GROUP_W, tm), tsd(GROUP_W, tm), tsd(GROUP_W, seq), tsd(GROUP_W, seq),
                      ff(R, LANES), tsd(IDX_DIM, seq), jax.ShapeDtypeStruct((batch, GROUP_W, seq), BF16)]
        out_specs += [kept(GROUP_W), kept(GROUP_W), full(GROUP_W), full(GROUP_W), blk(LANES), full(IDX_DIM),
                      pl.BlockSpec((1, GROUP_W, tm), lambda i: (i // keep_tiles, 0, i % keep_tiles))]
        if carried is not None:
            carried_out = (8, 9, 10, 11, 13)
            aliases = {len(args) + k: o for k, o in enumerate(carried_out)}
            args += list(carried)
            in_specs += [pl.BlockSpec(memory_space=pl.ANY)] * len(carried)
    return pl.pallas_call(
        functools.partial(_proj_kernel, keep_tiles=keep_tiles, transposed=stacked is not None,
                          n_alias=len(aliases)),
        grid=(n_tiles,),
        in_specs=in_specs,
        out_specs=out_specs,
        out_shape=out_shape,
        input_output_aliases=aliases,
        compiler_params=_cparams(1),
        name="proj_rope",
    )(*args)


def _head_masks(rows):
    lane = lax.broadcasted_iota(I32, (rows, LANES), 1)
    return [jnp.where((lane >= h * HEAD_DIM) & (lane < (h + 1) * HEAD_DIM), 1.0, 0.0).astype(BF16)
            for h in range(HEADS_PER_SLAB)]


def _band_kernel(q_ref, k_ref, v_ref, bias_ref, o_ref, *, n_groups, group_chunks, q_off):
    gq = group_chunks * CHUNK
    win = A_PAST + gq
    lane = lax.broadcasted_iota(I32, (gq, LANES), 1)
    head_masks = _head_masks(gq)
    for gi in range(n_groups):
        qs = gi * gq
        start = q_off + qs - A_PAST
        lo = max(start, 0)
        hi = q_off + qs + gq
        q2 = q_ref[0, qs:qs + gq, :]
        kw = k_ref[0, lo:hi, :]
        vw = v_ref[0, lo:hi, :]
        qst = jnp.concatenate([q2 * head_masks[h] for h in range(HEADS_PER_SLAB)], axis=0)
        s = lax.dot_general(qst, kw, (((1,), (1,)), ((), ())), preferred_element_type=F32)
        s = s + bias_ref[0, :, lo - start:win]
        m = jnp.max(s, axis=1, keepdims=True)
        p = jnp.exp(s - m)
        l = jnp.sum(p, axis=1, keepdims=True)
        o = jnp.dot(p.astype(BF16), vw, preferred_element_type=F32) / l
        o2 = jnp.where(lane < HEAD_DIM, o[:gq], o[gq:])
        o_ref[0, qs:qs + gq, :] = o2.astype(BF16)


def _band_attention(q, k, v, bias, q_off, group_chunks):
    B, Sq, _ = q.shape
    Sk = k.shape[1]
    gq = group_chunks * CHUNK
    return pl.pallas_call(
        functools.partial(_band_kernel, n_groups=Sq // gq, group_chunks=group_chunks, q_off=q_off),
        grid=(B, N_SLABS),
        in_specs=[pl.BlockSpec((1, Sq, LANES), lambda b, g: (b, 0, g)),
                  pl.BlockSpec((1, Sk, LANES), lambda b, g: (b, 0, g)),
                  pl.BlockSpec((1, Sk, LANES), lambda b, g: (b, 0, g)),
                  pl.BlockSpec((1, HEADS_PER_SLAB * gq, A_PAST + gq), lambda b, g: (g, 0, 0))],
        out_specs=pl.BlockSpec((1, Sq, LANES), lambda b, g: (b, 0, g)),
        out_shape=jax.ShapeDtypeStruct((B, Sq, GROUP_W), BF16),
        compiler_params=_cparams(2),
        name="band_attention",
    )(q, k, v, bias)


def _key_tiles(a, axis):
    step = LANES if axis == 1 else PACKED_ROWS
    n = a.shape[axis] // step
    return [a[:, t * step:(t + 1) * step] if axis == 1 else a[t * step:(t + 1) * step, :] for t in range(n)]


def _per_query(a, axis):
    return jnp.broadcast_to(a, (a.shape[0], LANES) if axis == 1 else (PACKED_ROWS, a.shape[1]))


def _count16(vals, cands, strict, axis):
    c16s = [_per_query(c, axis).astype(I16) for c in cands]
    tiles = _key_tiles(vals, axis)
    assert len(tiles) < 2 ** 8
    accs = [[None] * COUNT_ACCUMULATORS for _ in cands]
    for t, v in enumerate(tiles):
        for acc, c16 in zip(accs, c16s):
            hit = jnp.where((v > c16) if strict else (v >= c16), jnp.ones((), BF16), jnp.zeros((), BF16))
            slot = t % COUNT_ACCUMULATORS
            acc[slot] = hit if acc[slot] is None else acc[slot] + hit
    totals = [functools.reduce(lambda a, b: a + b, [a for a in acc if a is not None]) for acc in accs]
    return [jnp.sum(tot.astype(F32), axis=axis, keepdims=True) for tot in totals]


def _kth_largest16(vals, need, axis):
    bias = np.int32(2 ** 15)
    n_rounds = SEARCH_BITS // 2

    def search_round(it, u):
        b0 = np.int32(1) << (np.int32(2) * (np.int32(n_rounds - 1) - it))
        b1 = b0 + b0
        ok1, ok3, ok0 = (n >= need for n in
                         _count16(vals, [u + c - bias for c in (b1, b1 + b0, b0)], False, axis))
        return jnp.where(ok1, jnp.where(ok3, u + (b1 + b0), u + b1), jnp.where(ok0, u + b0, u))

    return lax.fori_loop(0, n_rounds, search_round, jnp.zeros(need.shape, I32)) - bias


def _kth_largest_key(key, kf, axis):
    hi = (key >> 16).astype(I16)
    lo = ((key & np.int32(0xFFFF)) - np.int32(2 ** 15)).astype(I16)
    need_hi = jnp.full((key.shape[0], 1) if axis == 1 else (1, key.shape[1]), kf, F32)
    p_hi = _kth_largest16(hi, need_hi, axis)
    need_lo = kf - _count16(hi, [p_hi], True, axis)[0]
    p16 = _per_query(p_hi, axis).astype(I16)
    lo_in_bucket = jnp.concatenate(
        [jnp.where(h == p16, l, np.int16(-2 ** 15)) for h, l in zip(_key_tiles(hi, axis), _key_tiles(lo, axis))],
        axis=axis)
    p_lo = _kth_largest16(lo_in_bucket, need_lo, axis)
    return (p_hi << 16) + (p_lo + np.int32(2 ** 15))


def _admissible(tq, W, q_pos0, keys_on_rows=False):
    shape, q_ax, k_ax = ((W, tq), 1, 0) if keys_on_rows else ((tq, W), 0, 1)
    q_idx = lax.broadcasted_iota(I32, shape, q_ax)
    k_idx = lax.broadcasted_iota(I32, shape, k_ax)
    chunk_shift = CHUNK.bit_length() - 1
    return (k_idx >> chunk_shift) <= ((q_idx + q_pos0) >> chunk_shift)


def _topk_negmask(qi_ref, ki2_ref, small_ref, *, width, q_pos0, topk, keys_on_rows):
    tq = qi_ref.shape[1]
    W = width
    axis = 0 if keys_on_rows else 1
    head_masks = _head_masks(tq)

    small = small_ref[0]
    qi = qi_ref[0]
    ki2 = ki2_ref[0, :W, :]
    masked = [qi[:, g * LANES:(g + 1) * LANES] * head_masks[h]
              for g in range(N_SLABS) for h in range(HEADS_PER_SLAB)]
    if keys_on_rows:
        dots = jnp.dot(ki2, jnp.concatenate([m.T for m in masked], axis=1), preferred_element_type=F32)
        weights = small.T
        head_dots = lambda h: dots[:, h * tq:(h + 1) * tq]
        head_w = lambda h: weights[IDX_DIM + h:IDX_DIM + h + 1, :]
    else:
        dots = lax.dot_general(jnp.concatenate(masked, axis=0), ki2, (((1,), (1,)), ((), ())),
                               preferred_element_type=F32)
        head_dots = lambda h: dots[h * tq:(h + 1) * tq, :]
        head_w = lambda h: small[:, IDX_DIM + h:IDX_DIM + h + 1]
    score = head_w(0) * jnp.maximum(head_dots(0), 0.0)
    for h in range(1, IDX_HEADS):
        score = score + head_w(h) * jnp.maximum(head_dots(h), 0.0)

    adm = _admissible(tq, W, q_pos0, keys_on_rows)
    score = jnp.where(adm, score + 0.0, -jnp.inf)

    bits = pltpu.bitcast(score, I32)
    key = bits ^ ((bits >> 31) & np.int32(0x7FFFFFFF))
    kf = np.float32(topk)
    thr = _kth_largest_key(key, kf, axis)

    thr = jnp.maximum(thr, KEY_NEG_INF + np.int32(1))
    ge = key >= thr
    n_ge = jnp.sum(jnp.where(ge, 1.0, 0.0), axis=axis, keepdims=True)
    sel_simple = jnp.where(ge, 0.0, -jnp.inf)

    def tie_path():
        gt = key > thr
        eq = key == thr
        need = kf - jnp.sum(jnp.where(gt, 1.0, 0.0), axis=axis, keepdims=True)
        eqf = jnp.where(eq, 1.0, 0.0).astype(BF16)
        ui = lax.broadcasted_iota(I32, (LANES, LANES), 0)
        uj = lax.broadcasted_iota(I32, (LANES, LANES), 1)
        tri = jnp.where((uj < ui) if keys_on_rows else (ui < uj), 1.0, 0.0).astype(BF16)
        offset = jnp.zeros_like(need)
        pieces = []
        for t in range(W // LANES):
            if keys_on_rows:
                e = eqf[t * LANES:(t + 1) * LANES, :]
                within = jnp.dot(tri, e, preferred_element_type=F32)
            else:
                e = eqf[:, t * LANES:(t + 1) * LANES]
                within = jnp.dot(e, tri, preferred_element_type=F32)
            pieces.append(offset + within < need)
            offset = offset + jnp.sum(e.astype(F32), axis=axis, keepdims=True)
        tie_ok = jnp.concatenate(pieces, axis=axis)
        return jnp.where(gt | (eq & tie_ok), 0.0, -jnp.inf)

    any_excess = jnp.max(n_ge) > kf
    negmask = lax.cond(any_excess, tie_path, lambda: sel_simple)
    return negmask.T if keys_on_rows else negmask


def _masked_attention(qb_ref, kb_ref, vb_ref, o_ref, negmask, *, width, k_transposed):
    tq = qb_ref.shape[1]
    W = width
    lane = lax.broadcasted_iota(I32, (tq, LANES), 1)
    head_masks = _head_masks(tq)
    qb = qb_ref[0]
    mask2 = jnp.concatenate([negmask] * HEADS_PER_SLAB, axis=0)
    for g in range(N_SLABS):
        q2 = qb[:, g * LANES:(g + 1) * LANES]
        v2 = vb_ref[0, :W, g * LANES:(g + 1) * LANES]
        qst = jnp.concatenate([q2 * head_masks[h] for h in range(HEADS_PER_SLAB)], axis=0)
        if k_transposed:
            s = jnp.dot(qst, kb_ref[0, g * LANES:(g + 1) * LANES, :W], preferred_element_type=F32)
        else:
            s = lax.dot_general(qst, kb_ref[0, :W, g * LANES:(g + 1) * LANES], (((1,), (1,)), ((), ())),
                                preferred_element_type=F32)
        s = s + mask2
        m = jnp.max(s, axis=1, keepdims=True)
        p = jnp.exp(s - m)
        l = jnp.sum(p, axis=1, keepdims=True)
        o = jnp.dot(p.astype(BF16), v2, preferred_element_type=F32) / l
        o2 = jnp.where(lane < HEAD_DIM, o[:tq], o[tq:])
        o_ref[0, :, g * LANES:(g + 1) * LANES] = o2.astype(BF16)


def _sparse_kernel(qb_ref, kb_ref, vb_ref, qi_ref, ki2_ref, small_ref, o_ref, *,
                   widths, causal_blocks, q_pos_base, topk, k_transposed):
    tq = qb_ref.shape[1]
    j = pl.program_id(1)
    attend = functools.partial(_masked_attention, qb_ref, kb_ref, vb_ref, o_ref, k_transposed=k_transposed)

    def body(W, q_pos0):
        negmask = _topk_negmask(qi_ref, ki2_ref, small_ref, width=W, q_pos0=q_pos0, topk=topk,
                                keys_on_rows=(tq % LANES == 0))
        attend(negmask, width=W)

    if not causal_blocks:
        body(widths[0], q_pos_base)
        return
    q_pos0 = j * tq
    n_keys = (j + 1) * tq
    dense_w = -(-topk // LANES) * LANES

    @pl.when(n_keys <= topk)
    def _():
        attend(jnp.where(_admissible(tq, dense_w, q_pos0), 0.0, -jnp.inf), width=dense_w)

    lo = topk
    for W in widths:
        if W <= lo:
            continue

        @pl.when((n_keys > lo) & (n_keys <= W))
        def _(W=W):
            body(W, q_pos0)
        lo = max(W, topk)


def _sparse_attention(qb, kb, vb, qi, ki2, small, *, tq, widths, causal_blocks, q_pos_base, topk,
                      k_transposed=False):
    B, Sq, _ = qb.shape
    Lk = vb.shape[1]
    nq = Sq // tq
    qspec = lambda n: pl.BlockSpec((1, tq, n), lambda b, j: (b, j, 0))
    kspec = lambda n: pl.BlockSpec((1, Lk, n), lambda b, j: (b, 0, 0))
    k_in = pl.BlockSpec((1, GROUP_W, Lk), lambda b, j: (b, 0, 0)) if k_transposed else kspec(GROUP_W)
    return pl.pallas_call(
        functools.partial(_sparse_kernel, widths=widths, causal_blocks=causal_blocks,
                          q_pos_base=q_pos_base, topk=topk, k_transposed=k_transposed),
        grid=(B, nq),
        in_specs=[qspec(GROUP_W), k_in, kspec(GROUP_W), qspec(GROUP_W), kspec(LANES), qspec(LANES)],
        out_specs=qspec(GROUP_W),
        out_shape=jax.ShapeDtypeStruct((B, Sq, GROUP_W), BF16),
        compiler_params=_cparams(2),
        name="sparse_attention",
    )(qb, kb, vb, qi, ki2, small)


def _layer_norm(x, g, b):
    mu = jnp.mean(x, axis=1, keepdims=True)
    xc = x - mu
    var = jnp.mean(xc * xc, axis=1, keepdims=True)
    return xc * lax.rsqrt(var + LN_EPS) * g + b


def _post_kernel(x_ref, oa_ref, ob_ref, wo_ref, g1_ref, b1_ref, wu_ref, wd_ref, g2_ref, b2_ref, y_ref,
                 *, alpha, ff_chunk):
    x = x_ref[...]
    mix = jnp.dot(oa_ref[...], wo_ref[:GROUP_W, :], preferred_element_type=F32)
    mix = mix + jnp.dot(ob_ref[...], wo_ref[GROUP_W:, :], preferred_element_type=F32)
    x1 = _layer_norm(alpha * x + mix, g1_ref[...], b1_ref[...])
    x1b = x1.astype(BF16)
    d_ff = wu_ref.shape[1]
    ff = jnp.zeros_like(x1)
    for c in range(d_ff // ff_chunk):
        hid = jnp.dot(x1b, wu_ref[:, c * ff_chunk:(c + 1) * ff_chunk], preferred_element_type=F32)
        hid = jnp.maximum(hid, 0.0)
        hid = (hid * hid).astype(BF16)
        ff = ff + jnp.dot(hid, wd_ref[c * ff_chunk:(c + 1) * ff_chunk, :], preferred_element_type=F32)
    y_ref[...] = _layer_norm(alpha * x1 + ff, g2_ref[...], b2_ref[...])


def _post(x2d, oa, ob, wo, g1, b1, wu, wd, g2, b2, alpha):
    R, D = x2d.shape
    tm = ROW_TILE
    d_ff = wu.shape[1]
    row = lambda i: (i, 0)
    const = lambda i: (0, 0)
    return pl.pallas_call(
        functools.partial(_post_kernel, alpha=alpha, ff_chunk=1024),
        grid=(R // tm,),
        in_specs=[pl.BlockSpec((tm, D), row),
                  pl.BlockSpec((tm, GROUP_W), row),
                  pl.BlockSpec((tm, GROUP_W), row),
                  pl.BlockSpec((2 * GROUP_W, D), const),
                  pl.BlockSpec((1, D), const), pl.BlockSpec((1, D), const),
                  pl.BlockSpec((D, d_ff), const),
                  pl.BlockSpec((d_ff, D), const),
                  pl.BlockSpec((1, D), const), pl.BlockSpec((1, D), const)],
        out_specs=pl.BlockSpec((tm, D), row),
        out_shape=jax.ShapeDtypeStruct((R, D), F32),
        compiler_params=_cparams(1),
        name="post_sublayers",
    )(x2d, oa, ob, wo, g1, b1, wu, wd, g2, b2)


def _rope_tables(pos):
    half = HEAD_DIM // 2
    inv = ROPE_THETA ** (-jnp.arange(half, dtype=F32) / half)
    ang = pos.astype(F32)[:, None] * inv[None, :]
    cos, sin = jnp.cos(ang), jnp.sin(ang)
    cos_h = jnp.concatenate([cos, cos], axis=1)
    sin_h = jnp.concatenate([-sin, sin], axis=1)
    return jnp.tile(cos_h, (1, HEADS_PER_SLAB)), jnp.tile(sin_h, (1, HEADS_PER_SLAB))


def _band_bias(table, group_chunks):
    gq = group_chunks * CHUNK
    win = A_PAST + gq
    d_min, d_max = A_PAST - (win - 1), gq - 1 + A_PAST
    n = d_max - d_min + 1
    left = jnp.repeat(table[:, :1], max(0, -REL_CLIP - d_min), axis=1)
    mid = table[:, max(d_min, -REL_CLIP) + REL_CLIP:min(d_max, REL_CLIP) + REL_CLIP + 1]
    right = jnp.repeat(table[:, -1:], max(0, d_max - REL_CLIP), axis=1)
    ext = jnp.concatenate([left, mid, right], axis=1).astype(F32)
    rp = jnp.concatenate([ext[:, ::-1], jnp.zeros((N_HEADS, 1), F32)], axis=1)
    shifted = jnp.tile(rp, (1, gq))[:, :gq * n].reshape(N_HEADS, gq, n)
    bias = shifted[:, :, gq - 1:gq - 1 + win]
    q_sub = jnp.arange(gq)[:, None] // CHUNK
    k_rel = jnp.arange(win)[None, :] - q_sub * CHUNK
    in_band = (k_rel >= 0) & (k_rel < A_BAND)
    bias = jnp.where(in_band[None], bias, -jnp.inf)
    return bias.reshape(N_SLABS, HEADS_PER_SLAB * gq, win)


def _dup_lanes(ki):
    return jnp.concatenate([ki, ki], axis=-1)


def kernel(x_prompt, x_sample, cache_a_k, cache_a_v, cache_b_k, cache_b_v, cache_b_kidx,
           w_in, rel_bias, w_out, ln1_g, ln1_b, w_up, w_down, ln2_g, ln2_b):
    B, S, D = x_prompt.shape
    Bs, T, _ = x_sample.shape
    depth = w_in.shape[0]
    p_len = cache_b_k.shape[2]
    a_len = cache_a_k.shape[2]
    assert S % ROW_TILE == 0 and (Bs * T) % ROW_TILE == 0 and T == CHUNK and a_len == A_PAST
    alpha = float((2 * depth) ** 0.25)
    topk_p = min(TOPK_MAX, S // 4)
    topk_s = min(TOPK_MAX, (p_len + T) // 4)
    keep_a = min(A_PAST, S)
    assert keep_a == ROW_TILE
    tiles_per_seq = S // ROW_TILE

    cos_p, sin_p = _rope_tables(jnp.arange(S))
    pos_s = jnp.tile(p_len + jnp.arange(T), ROW_TILE // T)
    cos_s, sin_s = _rope_tables(pos_s)

    tq_p = 2 * CHUNK
    widths_p = tuple(range(SPARSE_WIDTH_STEP, 3 * S // 4 + 1, SPARSE_WIDTH_STEP)) + (S,)
    lk_s = -(-(p_len + T) // LANES) * LANES
    pad_s = lk_s - (p_len + T)

    xp = x_prompt.reshape(B * S, D)
    xs = x_sample.reshape(Bs * T, D)
    outs = {k: [] for k in ("ak_s", "av_s", "bk_s", "bv_s", "bi_s")}
    carried = None

    for l in range(depth):
        w = jnp.pad(w_in[l], ((0, 0), (0, IN_PAD - w_in.shape[2]))).astype(BF16)
        wo = w_out[l].astype(BF16)
        wu = w_up[l].astype(BF16)
        wd = w_down[l].astype(BF16)
        g1, b1 = ln1_g[l][None, :], ln1_b[l][None, :]
        g2, b2 = ln2_g[l][None, :], ln2_b[l][None, :]
        bias_p = _band_bias(rel_bias[l], BAND_GROUP_CHUNKS)
        bias_s = _band_bias(rel_bias[l], 1)

        (qa, ka, va, qb, kb, vb, qi, ki2, *cache_p) = _project(xp, w, cos_p, sin_p, tiles_per_seq,
                                                                stacked=(l, depth, B, carried))
        small, kb_t = cache_p[4], cache_p[6]
        carried = cache_p[:4] + cache_p[5:6]
        r3 = lambda a: a.reshape(B, S, a.shape[-1])
        oa = _band_attention(r3(qa), r3(ka), r3(va), bias_p, q_off=0, group_chunks=BAND_GROUP_CHUNKS)
        ob = _sparse_attention(r3(qb), kb_t, r3(vb), r3(qi), r3(ki2), r3(small), tq=tq_p,
                               widths=widths_p, causal_blocks=True, q_pos_base=0, topk=topk_p,
                               k_transposed=True)
        xp = _post(xp, oa.reshape(B * S, GROUP_W), ob.reshape(B * S, GROUP_W), wo, g1, b1, wu, wd, g2, b2, alpha)

        (qa, ka, va, qb, kb, vb, qi, ki2, kaf, vaf, kbf, vbf, small) = _project(xs, w, cos_s, sin_s, 1)
        s3 = lambda a: a.reshape(Bs, T, a.shape[-1])
        cat_a = lambda c, n: jnp.concatenate([c.reshape(Bs, a_len, GROUP_W).astype(BF16), s3(n)], axis=1)
        oa = _band_attention(s3(qa), cat_a(cache_a_k[l], ka), cat_a(cache_a_v[l], va), bias_s,
                             q_off=a_len, group_chunks=1)
        zpad = lambda n: jnp.zeros((Bs, pad_s, n), BF16)
        cat_b = lambda c, n: jnp.concatenate(
            [c.reshape(Bs, p_len, GROUP_W).astype(BF16), s3(n), zpad(GROUP_W)], axis=1)
        ki2_all = jnp.concatenate([_dup_lanes(cache_b_kidx[l]).astype(BF16), s3(ki2), zpad(LANES)], axis=1)
        ob = _sparse_attention(s3(qb), cat_b(cache_b_k[l], kb), cat_b(cache_b_v[l], vb), s3(qi), ki2_all,
                               s3(small), tq=T, widths=(lk_s,), causal_blocks=False, q_pos_base=p_len,
                               topk=topk_s)
        xs = _post(xs, oa.reshape(Bs * T, GROUP_W), ob.reshape(Bs * T, GROUP_W), wo, g1, b1, wu, wd, g2, b2, alpha)
        outs["ak_s"].append(kaf.reshape(Bs, T, N_HEADS, HEAD_DIM))
        outs["av_s"].append(vaf.reshape(Bs, T, N_HEADS, HEAD_DIM))
        outs["bk_s"].append(kbf.reshape(Bs, T, N_HEADS, HEAD_DIM))
        outs["bv_s"].append(vbf.reshape(Bs, T, N_HEADS, HEAD_DIM))
        outs["bi_s"].append(small[:, :IDX_DIM].reshape(Bs, T, IDX_DIM))

    ak_t, av_t, bk_t, bv_t, bi_t = carried
    heads_last = lambda a: a.reshape(depth, B, N_HEADS, HEAD_DIM, a.shape[-1]).transpose(0, 1, 4, 2, 3)
    st = lambda k: jnp.stack(outs[k])
    return (xp.reshape(B, S, D), xs.reshape(Bs, T, D),
            heads_last(ak_t), heads_last(av_t), heads_last(bk_t), heads_last(bv_t), bi_t.transpose(0, 1, 3, 2),
            st("ak_s"), st("av_s"), st("bk_s"), st("bv_s"), st("bi_s"))
```

```python
import functools

import jax
import jax.numpy as jnp
import numpy as np
from jax import lax
from jax.experimental import pallas as pl
from jax.experimental.pallas import tpu as pltpu

F32 = jnp.float32
BF16 = jnp.bfloat16
I32 = jnp.int32
I16 = jnp.int16

CHUNK = 64
HEAD_DIM = 64
N_HEADS = 8
GROUP_W = N_HEADS * HEAD_DIM
A_PAST = 8 * CHUNK
A_BAND = A_PAST + CHUNK
REL_CLIP = 128
IDX_HEADS = 8
IDX_DIM = 64
TOPK_MAX = 256
ROPE_THETA = 10000.0
LN_EPS = 1e-5

LANES = 128
PACKED_ROWS = 16
HEADS_PER_SLAB = LANES // HEAD_DIM
N_SLABS = GROUP_W // LANES
ROW_TILE = 512
BAND_GROUP_CHUNKS = 4
SEARCH_BITS = 16
COUNT_ACCUMULATORS = 4
SPARSE_WIDTH_STEP = 256
VMEM_LIMIT = 56 * 1024 * 1024

IN_MAIN = 7 * GROUP_W
IN_SMALL0 = IN_MAIN
IN_PAD = IN_MAIN + LANES

KEY_NEG_INF = np.int32(np.int32(np.float32(-np.inf).view(np.int32)) ^ np.int32(0x7FFFFFFF))


def _cparams(n_axes):
    return pltpu.CompilerParams(dimension_semantics=("arbitrary",) * n_axes,
                                vmem_limit_bytes=VMEM_LIMIT)


def _proj_kernel(*refs, keep_tiles, transposed, n_alias):
    x_ref, w_ref, cos_ref, sin_ref = refs[:4]
    outs = refs[4 + n_alias:]
    qa_o, ka_o, va_o, qb_o, kb_o, vb_o, qi_o, ki2_o, kaf_o, vaf_o, kbf_o, vbf_o, small_o = outs[:13]
    i = pl.program_id(0)
    tm = x_ref.shape[0]
    x = x_ref[...].astype(BF16)
    cos = cos_ref[...]
    sin = sin_ref[...]
    lane = lax.broadcasted_iota(I32, (tm, LANES), 1)
    first_half = (lane & (HEAD_DIM - 1)) < (HEAD_DIM // 2)

    def sec(c0, n):
        return jnp.dot(x, w_ref[:, c0:c0 + n], preferred_element_type=F32)

    def rope_slab(slab):
        swapped = jnp.where(first_half, pltpu.roll(slab, LANES - HEAD_DIM // 2, 1),
                            pltpu.roll(slab, HEAD_DIM // 2, 1))
        return slab * cos + swapped * sin

    def rope(h):
        return jnp.concatenate([rope_slab(h[:, s * LANES:(s + 1) * LANES])
                                for s in range(h.shape[1] // LANES)], axis=1)

    def put_cache(o_ref, val):
        if transposed:
            o_ref[0, 0] = val.T
        else:
            o_ref[...] = val

    qa_o[...] = (sec(0 * GROUP_W, GROUP_W) * (HEAD_DIM ** -0.5)).astype(BF16)
    ka = sec(1 * GROUP_W, GROUP_W)
    ka_o[...] = ka.astype(BF16)
    va = sec(2 * GROUP_W, GROUP_W)
    va_o[...] = va.astype(BF16)

    put_cache(kaf_o, ka)
    put_cache(vaf_o, va)

    qb_o[...] = (rope(sec(3 * GROUP_W, GROUP_W)) * (HEAD_DIM ** -0.5)).astype(BF16)
    kb = rope(sec(4 * GROUP_W, GROUP_W))
    kb_o[...] = kb.astype(BF16)
    if transposed:
        kb_t = kb.T
        kbf_o[0, 0] = kb_t
        outs[14][0] = kb_t.astype(BF16)
    else:
        kbf_o[...] = kb
    vb = sec(5 * GROUP_W, GROUP_W)
    put_cache(vbf_o, vb)
    vb_o[...] = vb.astype(BF16)
    qi_o[...] = rope(sec(6 * GROUP_W, GROUP_W)).astype(BF16)

    sm = sec(IN_SMALL0, LANES)
    ki = rope_slab(sm)
    is_ki = lane < IDX_DIM
    small_o[...] = jnp.where(is_ki, ki, sm * ((IDX_HEADS * IDX_DIM) ** -0.5))
    ki2_o[...] = jnp.where(is_ki, ki, pltpu.roll(ki, IDX_DIM, 1)).astype(BF16)
    if transposed:
        outs[13][0, 0] = ki.T[:IDX_DIM, :]


def _project(x2d, w, cos_t, sin_t, keep_tiles, stacked=None):
    R, D = x2d.shape
    tm = ROW_TILE
    n_tiles = R // tm
    n_pos = cos_t.shape[0] // tm
    row = lambda i: (i, 0)
    bf = lambda n: jax.ShapeDtypeStruct((R, n), BF16)
    ff = lambda r, n: jax.ShapeDtypeStruct((r, n), F32)
    blk = lambda n: pl.BlockSpec((tm, n), row)
    out_shape = [bf(GROUP_W)] * 7 + [bf(LANES)]
    out_specs = [blk(GROUP_W)] * 7 + [blk(LANES)]
    args = [x2d, w, cos_t, sin_t]
    in_specs = [pl.BlockSpec((tm, D), row),
                pl.BlockSpec((D, IN_PAD), lambda i: (0, 0)),
                pl.BlockSpec((tm, LANES), lambda i: (i % n_pos, 0)),
                pl.BlockSpec((tm, LANES), lambda i: (i % n_pos, 0))]
    aliases = {}
    if stacked is None:
        keep_blk = pl.BlockSpec((tm, GROUP_W), lambda i: (i // keep_tiles, 0))
        out_shape += [ff(R // keep_tiles, GROUP_W), ff(R // keep_tiles, GROUP_W),
                      ff(R, GROUP_W), ff(R, GROUP_W), ff(R, LANES)]
        out_specs += [keep_blk, keep_blk, blk(GROUP_W), blk(GROUP_W), blk(LANES)]
    else:
        layer, depth, batch, carried = stacked
        seq = keep_tiles * tm
        tsd = lambda n, t: jax.ShapeDtypeStruct((depth, batch, n, t), F32)
        kept = lambda n: pl.BlockSpec((1, 1, n, tm), lambda i: (layer, i // keep_tiles, 0, 0))
        full = lambda n: pl.BlockSpec((1, 1, n, tm), lambda i: (layer, i // keep_tiles, 0, i % keep_tiles))
        out_shape += [tsd(GROUP_W, tm), tsd(GROUP_W, tm), tsd(GROUP_W, seq), tsd(GROUP_W, seq),
                      ff(R, LANES), tsd(IDX_DIM, seq), jax.ShapeDtypeStruct((batch, GROUP_W, seq), BF16)]
        out_specs += [kept(GROUP_W), kept(GROUP_W), full(GROUP_W), full(GROUP_W), blk(LANES), full(IDX_DIM),
                      pl.BlockSpec((1, GROUP_W, tm), lambda i: (i // keep_tiles, 0, i % keep_tiles))]
        if carried is not None:
            carried_out = (8, 9, 10, 11, 13)
            aliases = {len(args) + k: o for k, o in enumerate(carried_out)}
            args += list(carried)
            in_specs += [pl.BlockSpec(memory_space=pl.ANY)] * len(carried)
    return pl.pallas_call(
        functools.partial(_proj_kernel, keep_tiles=keep_tiles, transposed=stacked is not None,
                          n_alias=len(aliases)),
        grid=(n_tiles,),
        in_specs=in_specs,
        out_specs=out_specs,
        out_shape=out_shape,
        input_output_aliases=aliases,
        compiler_params=_cparams(1),
        name="proj_rope",
    )(*args)


def _head_masks(rows):
    lane = lax.broadcasted_iota(I32, (rows, LANES), 1)
    return [jnp.where((lane >= h * HEAD_DIM) & (lane < (h + 1) * HEAD_DIM), 1.0, 0.0).astype(BF16)
            for h in range(HEADS_PER_SLAB)]


def _band_kernel(q_ref, k_ref, v_ref, bias_ref, o_ref, *, n_groups, group_chunks, q_off):
    gq = group_chunks * CHUNK
    win = A_PAST + gq
    lane = lax.broadcasted_iota(I32, (gq, LANES), 1)
    head_masks = _head_masks(gq)
    for gi in range(n_groups):
        qs = gi * gq
        start = q_off + qs - A_PAST
        lo = max(start, 0)
        hi = q_off + qs + gq
        q2 = q_ref[0, qs:qs + gq, :]
        kw = k_ref[0, lo:hi, :]
        vw = v_ref[0, lo:hi, :]
        qst = jnp.concatenate([q2 * head_masks[h] for h in range(HEADS_PER_SLAB)], axis=0)
        s = lax.dot_general(qst, kw, (((1,), (1,)), ((), ())), preferred_element_type=F32)
        s = s + bias_ref[0, :, lo - start:win]
        m = jnp.max(s, axis=1, keepdims=True)
        p = jnp.exp(s - m)
        l = jnp.sum(p, axis=1, keepdims=True)
        o = jnp.dot(p.astype(BF16), vw, preferred_element_type=F32) / l
        o2 = jnp.where(lane < HEAD_DIM, o[:gq], o[gq:])
        o_ref[0, qs:qs + gq, :] = o2.astype(BF16)


def _band_attention(q, k, v, bias, q_off, group_chunks):
    B, Sq, _ = q.shape
    Sk = k.shape[1]
    gq = group_chunks * CHUNK
    return pl.pallas_call(
        functools.partial(_band_kernel, n_groups=Sq // gq, group_chunks=group_chunks, q_off=q_off),
        grid=(B, N_SLABS),
        in_specs=[pl.BlockSpec((1, Sq, LANES), lambda b, g: (b, 0, g)),
                  pl.BlockSpec((1, Sk, LANES), lambda b, g: (b, 0, g)),
                  pl.BlockSpec((1, Sk, LANES), lambda b, g: (b, 0, g)),
                  pl.BlockSpec((1, HEADS_PER_SLAB * gq, A_PAST + gq), lambda b, g: (g, 0, 0))],
        out_specs=pl.BlockSpec((1, Sq, LANES), lambda b, g: (b, 0, g)),
        out_shape=jax.ShapeDtypeStruct((B, Sq, GROUP_W), BF16),
        compiler_params=_cparams(2),
        name="band_attention",
    )(q, k, v, bias)


def _key_tiles(a, axis):
    step = LANES if axis == 1 else PACKED_ROWS
    n = a.shape[axis] // step
    return [a[:, t * step:(t + 1) * step] if axis == 1 else a[t * step:(t + 1) * step, :] for t in range(n)]


def _per_query(a, axis):
    return jnp.broadcast_to(a, (a.shape[0], LANES) if axis == 1 else (PACKED_ROWS, a.shape[1]))


def _count16(vals, cands, strict, axis):
    c16s = [_per_query(c, axis).astype(I16) for c in cands]
    tiles = _key_tiles(vals, axis)
    assert len(tiles) < 2 ** 8
    accs = [[None] * COUNT_ACCUMULATORS for _ in cands]
    for t, v in enumerate(tiles):
        for acc, c16 in zip(accs, c16s):
            hit = jnp.where((v > c16) if strict else (v >= c16), jnp.ones((), BF16), jnp.zeros((), BF16))
            slot = t % COUNT_ACCUMULATORS
            acc[slot] = hit if acc[slot] is None else acc[slot] + hit
    totals = [functools.reduce(lambda a, b: a + b, [a for a in acc if a is not None]) for acc in accs]
    return [jnp.sum(tot.astype(F32), axis=axis, keepdims=True) for tot in totals]


def _kth_largest16(vals, need, axis):
    bias = np.int32(2 ** 15)

    if axis == 0:
        def bit_round(it, u):
            b0 = np.int32(1) << (np.int32(SEARCH_BITS - 1) - it)
            ok, = _count16(vals, [u + b0 - bias], False, axis)
            return jnp.where(ok >= need, u + b0, u)

        return lax.fori_loop(0, SEARCH_BITS, bit_round, jnp.zeros(need.shape, I32)) - bias

    n_rounds = SEARCH_BITS // 2

    def search_round(it, u):
        b0 = np.int32(1) << (np.int32(2) * (np.int32(n_rounds - 1) - it))
        b1 = b0 + b0
        ok1, ok3, ok0 = (n >= need for n in
                         _count16(vals, [u + c - bias for c in (b1, b1 + b0, b0)], False, axis))
        return jnp.where(ok1, jnp.where(ok3, u + (b1 + b0), u + b1), jnp.where(ok0, u + b0, u))

    return lax.fori_loop(0, n_rounds, search_round, jnp.zeros(need.shape, I32)) - bias


def _kth_largest_key(key, kf, axis):
    hi = (key >> 16).astype(I16)
    lo = ((key & np.int32(0xFFFF)) - np.int32(2 ** 15)).astype(I16)
    need_hi = jnp.full((key.shape[0], 1) if axis == 1 else (1, key.shape[1]), kf, F32)
    p_hi = _kth_largest16(hi, need_hi, axis)
    need_lo = kf - _count16(hi, [p_hi], True, axis)[0]
    p16 = _per_query(p_hi, axis).astype(I16)
    lo_in_bucket = jnp.concatenate(
        [jnp.where(h == p16, l, np.int16(-2 ** 15)) for h, l in zip(_key_tiles(hi, axis), _key_tiles(lo, axis))],
        axis=axis)
    p_lo = _kth_largest16(lo_in_bucket, need_lo, axis)
    return (p_hi << 16) + (p_lo + np.int32(2 ** 15))


def _admissible(tq, W, q_pos0, keys_on_rows=False):
    shape, q_ax, k_ax = ((W, tq), 1, 0) if keys_on_rows else ((tq, W), 0, 1)
    q_idx = lax.broadcasted_iota(I32, shape, q_ax)
    k_idx = lax.broadcasted_iota(I32, shape, k_ax)
    chunk_shift = CHUNK.bit_length() - 1
    return (k_idx >> chunk_shift) <= ((q_idx + q_pos0) >> chunk_shift)


def _topk_negmask(qi_ref, ki2_ref, small_ref, *, width, q_pos0, topk, keys_on_rows):
    tq = qi_ref.shape[1]
    W = width
    axis = 0 if keys_on_rows else 1
    head_masks = _head_masks(tq)

    small = small_ref[0]
    qi = qi_ref[0]
    ki2 = ki2_ref[0, :W, :]
    masked = [qi[:, g * LANES:(g + 1) * LANES] * head_masks[h]
              for g in range(N_SLABS) for h in range(HEADS_PER_SLAB)]
    if keys_on_rows:
        dots = jnp.dot(ki2, jnp.concatenate([m.T for m in masked], axis=1), preferred_element_type=F32)
        weights = small.T
        head_dots = lambda h: dots[:, h * tq:(h + 1) * tq]
        head_w = lambda h: weights[IDX_DIM + h:IDX_DIM + h + 1, :]
    else:
        dots = lax.dot_general(jnp.concatenate(masked, axis=0), ki2, (((1,), (1,)), ((), ())),
                               preferred_element_type=F32)
        head_dots = lambda h: dots[h * tq:(h + 1) * tq, :]
        head_w = lambda h: small[:, IDX_DIM + h:IDX_DIM + h + 1]
    score = head_w(0) * jnp.maximum(head_dots(0), 0.0)
    for h in range(1, IDX_HEADS):
        score = score + head_w(h) * jnp.maximum(head_dots(h), 0.0)

    adm = _admissible(tq, W, q_pos0, keys_on_rows)
    score = jnp.where(adm, score + 0.0, -jnp.inf)

    bits = pltpu.bitcast(score, I32)
    key = bits ^ ((bits >> 31) & np.int32(0x7FFFFFFF))
    kf = np.float32(topk)
    thr = _kth_largest_key(key, kf, axis)

    thr = jnp.maximum(thr, KEY_NEG_INF + np.int32(1))
    ge = key >= thr
    n_ge = jnp.sum(jnp.where(ge, 1.0, 0.0), axis=axis, keepdims=True)
    sel_simple = jnp.where(ge, 0.0, -jnp.inf)

    def tie_path():
        gt = key > thr
        eq = key == thr
        need = kf - jnp.sum(jnp.where(gt, 1.0, 0.0), axis=axis, keepdims=True)
        eqf = jnp.where(eq, 1.0, 0.0).astype(BF16)
        ui = lax.broadcasted_iota(I32, (LANES, LANES), 0)
        uj = lax.broadcasted_iota(I32, (LANES, LANES), 1)
        tri = jnp.where((uj < ui) if keys_on_rows else (ui < uj), 1.0, 0.0).astype(BF16)
        offset = jnp.zeros_like(need)
        pieces = []
        for t in range(W // LANES):
            if keys_on_rows:
                e = eqf[t * LANES:(t + 1) * LANES, :]
                within = jnp.dot(tri, e, preferred_element_type=F32)
            else:
                e = eqf[:, t * LANES:(t + 1) * LANES]
                within = jnp.dot(e, tri, preferred_element_type=F32)
            pieces.append(offset + within < need)
            offset = offset + jnp.sum(e.astype(F32), axis=axis, keepdims=True)
        tie_ok = jnp.concatenate(pieces, axis=axis)
        return jnp.where(gt | (eq & tie_ok), 0.0, -jnp.inf)

    any_excess = jnp.max(n_ge) > kf
    negmask = lax.cond(any_excess, tie_path, lambda: sel_simple)
    return negmask.T if keys_on_rows else negmask


def _masked_attention(qb_ref, kb_ref, vb_ref, o_ref, negmask, *, width, k_transposed):
    tq = qb_ref.shape[1]
    W = width
    lane = lax.broadcasted_iota(I32, (tq, LANES), 1)
    head_masks = _head_masks(tq)
    qb = qb_ref[0]
    mask2 = jnp.concatenate([negmask] * HEADS_PER_SLAB, axis=0)
    for g in range(N_SLABS):
        q2 = qb[:, g * LANES:(g + 1) * LANES]
        v2 = vb_ref[0, :W, g * LANES:(g + 1) * LANES]
        qst = jnp.concatenate([q2 * head_masks[h] for h in range(HEADS_PER_SLAB)], axis=0)
        if k_transposed:
            s = jnp.dot(qst, kb_ref[0, g * LANES:(g + 1) * LANES, :W], preferred_element_type=F32)
        else:
            s = lax.dot_general(qst, kb_ref[0, :W, g * LANES:(g + 1) * LANES], (((1,), (1,)), ((), ())),
                                preferred_element_type=F32)
        s = s + mask2
        m = jnp.max(s, axis=1, keepdims=True)
        p = jnp.exp(s - m)
        l = jnp.sum(p, axis=1, keepdims=True)
        o = jnp.dot(p.astype(BF16), v2, preferred_element_type=F32) / l
        o2 = jnp.where(lane < HEAD_DIM, o[:tq], o[tq:])
        o_ref[0, :, g * LANES:(g + 1) * LANES] = o2.astype(BF16)


def _sparse_kernel(qb_ref, kb_ref, vb_ref, qi_ref, ki2_ref, small_ref, o_ref, *,
                   widths, causal_blocks, q_pos_base, topk, k_transposed):
    tq = qb_ref.shape[1]
    j = pl.program_id(1)
    attend = functools.partial(_masked_attention, qb_ref, kb_ref, vb_ref, o_ref, k_transposed=k_transposed)

    def body(W, q_pos0):
        negmask = _topk_negmask(qi_ref, ki2_ref, small_ref, width=W, q_pos0=q_pos0, topk=topk,
                                keys_on_rows=(tq % LANES == 0))
        attend(negmask, width=W)

    if not causal_blocks:
        body(widths[0], q_pos_base)
        return
    q_pos0 = j * tq
    n_keys = (j + 1) * tq
    dense_w = -(-topk // LANES) * LANES

    @pl.when(n_keys <= topk)
    def _():
        attend(jnp.where(_admissible(tq, dense_w, q_pos0), 0.0, -jnp.inf), width=dense_w)

    lo = topk
    for W in widths:
        if W <= lo:
            continue

        @pl.when((n_keys > lo) & (n_keys <= W))
        def _(W=W):
            body(W, q_pos0)
        lo = max(W, topk)


def _sparse_attention(qb, kb, vb, qi, ki2, small, *, tq, widths, causal_blocks, q_pos_base, topk,
                      k_transposed=False):
    B, Sq, _ = qb.shape
    Lk = vb.shape[1]
    nq = Sq // tq
    qspec = lambda n: pl.BlockSpec((1, tq, n), lambda b, j: (b, j, 0))
    kspec = lambda n: pl.BlockSpec((1, Lk, n), lambda b, j: (b, 0, 0))
    k_in = pl.BlockSpec((1, GROUP_W, Lk), lambda b, j: (b, 0, 0)) if k_transposed else kspec(GROUP_W)
    return pl.pallas_call(
        functools.partial(_sparse_kernel, widths=widths, causal_blocks=causal_blocks,
                          q_pos_base=q_pos_base, topk=topk, k_transposed=k_transposed),
        grid=(B, nq),
        in_specs=[qspec(GROUP_W), k_in, kspec(GROUP_W), qspec(GROUP_W), kspec(LANES), qspec(LANES)],
        out_specs=qspec(GROUP_W),
        out_shape=jax.ShapeDtypeStruct((B, Sq, GROUP_W), BF16),
        compiler_params=_cparams(2),
        name="sparse_attention",
    )(qb, kb, vb, qi, ki2, small)


def _layer_norm(x, g, b):
    mu = jnp.mean(x, axis=1, keepdims=True)
    xc = x - mu
    var = jnp.mean(xc * xc, axis=1, keepdims=True)
    return xc * lax.rsqrt(var + LN_EPS) * g + b


def _post_kernel(x_ref, oa_ref, ob_ref, wo_ref, g1_ref, b1_ref, wu_ref, wd_ref, g2_ref, b2_ref, y_ref,
                 *, alpha, ff_chunk):
    x = x_ref[...]
    mix = jnp.dot(oa_ref[...], wo_ref[:GROUP_W, :], preferred_element_type=F32)
    mix = mix + jnp.dot(ob_ref[...], wo_ref[GROUP_W:, :], preferred_element_type=F32)
    x1 = _layer_norm(alpha * x + mix, g1_ref[...], b1_ref[...])
    x1b = x1.astype(BF16)
    d_ff = wu_ref.shape[1]
    ff = jnp.zeros_like(x1)
    for c in range(d_ff // ff_chunk):
        hid = jnp.dot(x1b, wu_ref[:, c * ff_chunk:(c + 1) * ff_chunk], preferred_element_type=F32)
        hid = jnp.maximum(hid, 0.0)
        hid = (hid * hid).astype(BF16)
        ff = ff + jnp.dot(hid, wd_ref[c * ff_chunk:(c + 1) * ff_chunk, :], preferred_element_type=F32)
    y_ref[...] = _layer_norm(alpha * x1 + ff, g2_ref[...], b2_ref[...])


def _post(x2d, oa, ob, wo, g1, b1, wu, wd, g2, b2, alpha):
    R, D = x2d.shape
    tm = ROW_TILE
    d_ff = wu.shape[1]
    row = lambda i: (i, 0)
    const = lambda i: (0, 0)
    return pl.pallas_call(
        functools.partial(_post_kernel, alpha=alpha, ff_chunk=1024),
        grid=(R // tm,),
        in_specs=[pl.BlockSpec((tm, D), row),
                  pl.BlockSpec((tm, GROUP_W), row),
                  pl.BlockSpec((tm, GROUP_W), row),
                  pl.BlockSpec((2 * GROUP_W, D), const),
                  pl.BlockSpec((1, D), const), pl.BlockSpec((1, D), const),
                  pl.BlockSpec((D, d_ff), const),
                  pl.BlockSpec((d_ff, D), const),
                  pl.BlockSpec((1, D), const), pl.BlockSpec((1, D), const)],
        out_specs=pl.BlockSpec((tm, D), row),
        out_shape=jax.ShapeDtypeStruct((R, D), F32),
        compiler_params=_cparams(1),
        name="post_sublayers",
    )(x2d, oa, ob, wo, g1, b1, wu, wd, g2, b2)


def _rope_tables(pos):
    half = HEAD_DIM // 2
    inv = ROPE_THETA ** (-jnp.arange(half, dtype=F32) / half)
    ang = pos.astype(F32)[:, None] * inv[None, :]
    cos, sin = jnp.cos(ang), jnp.sin(ang)
    cos_h = jnp.concatenate([cos, cos], axis=1)
    sin_h = jnp.concatenate([-sin, sin], axis=1)
    return jnp.tile(cos_h, (1, HEADS_PER_SLAB)), jnp.tile(sin_h, (1, HEADS_PER_SLAB))


def _band_bias(table, group_chunks):
    gq = group_chunks * CHUNK
    win = A_PAST + gq
    d_min, d_max = A_PAST - (win - 1), gq - 1 + A_PAST
    n = d_max - d_min + 1
    left = jnp.repeat(table[:, :1], max(0, -REL_CLIP - d_min), axis=1)
    mid = table[:, max(d_min, -REL_CLIP) + REL_CLIP:min(d_max, REL_CLIP) + REL_CLIP + 1]
    right = jnp.repeat(table[:, -1:], max(0, d_max - REL_CLIP), axis=1)
    ext = jnp.concatenate([left, mid, right], axis=1).astype(F32)
    rp = jnp.concatenate([ext[:, ::-1], jnp.zeros((N_HEADS, 1), F32)], axis=1)
    shifted = jnp.tile(rp, (1, gq))[:, :gq * n].reshape(N_HEADS, gq, n)
    bias = shifted[:, :, gq - 1:gq - 1 + win]
    q_sub = jnp.arange(gq)[:, None] // CHUNK
    k_rel = jnp.arange(win)[None, :] - q_sub * CHUNK
    in_band = (k_rel >= 0) & (k_rel < A_BAND)
    bias = jnp.where(in_band[None], bias, -jnp.inf)
    return bias.reshape(N_SLABS, HEADS_PER_SLAB * gq, win)


def _dup_lanes(ki):
    return jnp.concatenate([ki, ki], axis=-1)


def kernel(x_prompt, x_sample, cache_a_k, cache_a_v, cache_b_k, cache_b_v, cache_b_kidx,
           w_in, rel_bias, w_out, ln1_g, ln1_b, w_up, w_down, ln2_g, ln2_b):
    B, S, D = x_prompt.shape
    Bs, T, _ = x_sample.shape
    depth = w_in.shape[0]
    p_len = cache_b_k.shape[2]
    a_len = cache_a_k.shape[2]
    assert S % ROW_TILE == 0 and (Bs * T) % ROW_TILE == 0 and T == CHUNK and a_len == A_PAST
    alpha = float((2 * depth) ** 0.25)
    topk_p = min(TOPK_MAX, S // 4)
    topk_s = min(TOPK_MAX, (p_len + T) // 4)
    keep_a = min(A_PAST, S)
    assert keep_a == ROW_TILE
    tiles_per_seq = S // ROW_TILE

    cos_p, sin_p = _rope_tables(jnp.arange(S))
    pos_s = jnp.tile(p_len + jnp.arange(T), ROW_TILE // T)
    cos_s, sin_s = _rope_tables(pos_s)

    tq_p = 2 * CHUNK
    widths_p = tuple(range(SPARSE_WIDTH_STEP, 3 * S // 4 + 1, SPARSE_WIDTH_STEP)) + (S,)
    lk_s = -(-(p_len + T) // LANES) * LANES
    pad_s = lk_s - (p_len + T)

    xp = x_prompt.reshape(B * S, D)
    xs = x_sample.reshape(Bs * T, D)
    outs = {k: [] for k in ("ak_s", "av_s", "bk_s", "bv_s", "bi_s")}
    carried = None

    for l in range(depth):
        w = jnp.pad(w_in[l], ((0, 0), (0, IN_PAD - w_in.shape[2]))).astype(BF16)
        wo = w_out[l].astype(BF16)
        wu = w_up[l].astype(BF16)
        wd = w_down[l].astype(BF16)
        g1, b1 = ln1_g[l][None, :], ln1_b[l][None, :]
        g2, b2 = ln2_g[l][None, :], ln2_b[l][None, :]
        bias_p = _band_bias(rel_bias[l], BAND_GROUP_CHUNKS)
        bias_s = _band_bias(rel_bias[l], 1)

        (qa, ka, va, qb, kb, vb, qi, ki2, *cache_p) = _project(xp, w, cos_p, sin_p, tiles_per_seq,
                                                                stacked=(l, depth, B, carried))
        small, kb_t = cache_p[4], cache_p[6]
        carried = cache_p[:4] + cache_p[5:6]
        r3 = lambda a: a.reshape(B, S, a.shape[-1])
        oa = _band_attention(r3(qa), r3(ka), r3(va), bias_p, q_off=0, group_chunks=BAND_GROUP_CHUNKS)
        ob = _sparse_attention(r3(qb), kb_t, r3(vb), r3(qi), r3(ki2), r3(small), tq=tq_p,
                               widths=widths_p, causal_blocks=True, q_pos_base=0, topk=topk_p,
                               k_transposed=True)
        xp = _post(xp, oa.reshape(B * S, GROUP_W), ob.reshape(B * S, GROUP_W), wo, g1, b1, wu, wd, g2, b2, alpha)

        (qa, ka, va, qb, kb, vb, qi, ki2, kaf, vaf, kbf, vbf, small) = _project(xs, w, cos_s, sin_s, 1)
        s3 = lambda a: a.reshape(Bs, T, a.shape[-1])
        cat_a = lambda c, n: jnp.concatenate([c.reshape(Bs, a_len, GROUP_W).astype(BF16), s3(n)], axis=1)
        oa = _band_attention(s3(qa), cat_a(cache_a_k[l], ka), cat_a(cache_a_v[l], va), bias_s,
                             q_off=a_len, group_chunks=1)
        zpad = lambda n: jnp.zeros((Bs, pad_s, n), BF16)
        cat_b = lambda c, n: jnp.concatenate(
            [c.reshape(Bs, p_len, GROUP_W).astype(BF16), s3(n), zpad(GROUP_W)], axis=1)
        ki2_all = jnp.concatenate([_dup_lanes(cache_b_kidx[l]).astype(BF16), s3(ki2), zpad(LANES)], axis=1)
        ob = _sparse_attention(s3(qb), cat_b(cache_b_k[l], kb), cat_b(cache_b_v[l], vb), s3(qi), ki2_all,
                               s3(small), tq=T, widths=(lk_s,), causal_blocks=False, q_pos_base=p_len,
                               topk=topk_s)
        xs = _post(xs, oa.reshape(Bs * T, GROUP_W), ob.reshape(Bs * T, GROUP_W), wo, g1, b1, wu, wd, g2, b2, alpha)
        outs["ak_s"].append(kaf.reshape(Bs, T, N_HEADS, HEAD_DIM))
        outs["av_s"].append(vaf.reshape(Bs, T, N_HEADS, HEAD_DIM))
        outs["bk_s"].append(kbf.reshape(Bs, T, N_HEADS, HEAD_DIM))
        outs["bv_s"].append(vbf.reshape(Bs, T, N_HEADS, HEAD_DIM))
        outs["bi_s"].append(small[:, :IDX_DIM].reshape(Bs, T, IDX_DIM))

    ak_t, av_t, bk_t, bv_t, bi_t = carried
    heads_last = lambda a: a.reshape(depth, B, N_HEADS, HEAD_DIM, a.shape[-1]).transpose(0, 1, 4, 2, 3)
    st = lambda k: jnp.stack(outs[k])
    return (xp.reshape(B, S, D), xs.reshape(Bs, T, D),
            heads_last(ak_t), heads_last(av_t), heads_last(bk_t), heads_last(bv_t), bi_t.transpose(0, 1, 3, 2),
            st("ak_s"), st("av_s"), st("bk_s"), st("bv_s"), st("bi_s"))
```
